```python
import math
import jax
import jax.numpy as jnp
from jax import lax
import numpy as np

D_MODEL = 2048
BATCH = 4
SEQ = 2048
DEPTH = 2
DEC_BATCH = 128
DEC_SEQ = 8
PAST_LEN = 2048
PAGE_SIZE = 128

N_A = DEPTH // 2
N_B = DEPTH - N_A
N_META = 16
EPS = 1e-6
L2_EPS = 1e-6
D_FF = ((8 * D_MODEL // 3 + 255) // 256) * 256
GDN_HEAD_K = 128
GDN_HEAD_V = 128
GDN_QK_HEADS = D_MODEL // 128
GDN_V_HEADS = 2 * GDN_QK_HEADS
GDN_GROUP = GDN_V_HEADS // GDN_QK_HEADS
GDN_CONV_W = 4
GDN_CHUNK = 64
GDN_QK_DIM = GDN_QK_HEADS * GDN_HEAD_K
GDN_V_DIM = GDN_V_HEADS * GDN_HEAD_V
GDN_CONV_DIM = 2 * GDN_QK_DIM + GDN_V_DIM
GDN_PROJ = GDN_CONV_DIM + GDN_V_DIM + 2 * GDN_V_HEADS
DIFF_HEAD = 128
DIFF_HEADS = D_MODEL // (2 * DIFF_HEAD)
N_MAPS = 2 * DIFF_HEADS
DIFF_QK_DIM = N_MAPS * DIFF_HEAD
DIFF_V_DIM = DIFF_HEADS * 2 * DIFF_HEAD
ATTN_SCALE = DIFF_HEAD ** -0.5
Q_BLOCK = 128
NUM_BUCKETS = 32
MAX_DISTANCE = 128

kernel_name = "yoco_gdn_diffattn_macaron_step"


def rmsnorm(x, g):
    xf = x.astype(jnp.float32)
    y = xf * lax.rsqrt(jnp.mean(xf * xf, axis=-1, keepdims=True) + EPS)
    return (y * g.astype(jnp.float32)).astype(x.dtype)


def l2norm(x):
    return x * lax.rsqrt(jnp.sum(x * x, axis=-1, keepdims=True) + L2_EPS)


def swiglu(x, w_in, w_out):
    gate, up = jnp.split(x @ w_in, 2, axis=-1)
    return (jax.nn.silu(gate) * up) @ w_out


def t5_bucket(dist):
    max_exact = NUM_BUCKETS // 2
    n = jnp.maximum(dist, 0)
    log_ratio = jnp.log(jnp.maximum(n, 1).astype(jnp.float32) / max_exact) / math.log(MAX_DISTANCE / max_exact)
    large = jnp.minimum(max_exact + (log_ratio * (NUM_BUCKETS - max_exact)).astype(jnp.int32), NUM_BUCKETS - 1)
    return jnp.where(n < max_exact, n, large)


def causal_conv(buf, u, w):
    full = jnp.concatenate([buf.astype(u.dtype), u], axis=1)
    t = u.shape[1]
    out = full[:, 0:t] * w[0]
    for i in range(1, GDN_CONV_W):
        out = out + full[:, i:i + t] * w[i]
    return jax.nn.silu(out), full[:, t:]


def gdn_chunk(s, q, k, v, beta, g):
    c = q.shape[2]
    dv = v.shape[-1]
    gc = jnp.cumsum(g, axis=-1)
    tril = jnp.tril(jnp.ones((c, c), bool))
    strict = jnp.tril(jnp.ones((c, c), bool), -1)
    decay = jnp.exp(jnp.where(tril, gc[..., :, None] - gc[..., None, :], -jnp.inf))
    kb = k * beta[..., None]
    a = jnp.where(strict, jnp.einsum('bhid,bhjd->bhij', kb, k) * decay, 0.0)
    rhs = jnp.concatenate([v * beta[..., None], kb * jnp.exp(gc)[..., None]], axis=-1)
    sol = lax.linalg.triangular_solve(jnp.eye(c, dtype=q.dtype) + a, rhs, left_side=True, lower=True, unit_diagonal=True)
    u, w = sol[..., :dv], sol[..., dv:]
    v_new = u - jnp.einsum('bhck,bhkv->bhcv', w, s)
    qk = jnp.einsum('bhid,bhjd->bhij', q, k) * decay
    o = jnp.einsum('bhck,bhkv->bhcv', q * jnp.exp(gc)[..., None], s) + jnp.einsum('bhij,bhjv->bhiv', qk, v_new)
    g_last = gc[..., -1:]
    s_new = s * jnp.exp(g_last)[..., None] + jnp.einsum('bhck,bhcv->bhkv', k * jnp.exp(g_last - gc)[..., None], v_new)
    return s_new, o


def gdn_recurrence(s, q, k, v, beta, g, lead):
    outs = []
    if lead > 0:
        s, o = gdn_chunk(s, q[:, :, :lead], k[:, :, :lead], v[:, :, :lead], beta[:, :, :lead], g[:, :, :lead])
        outs.append(o)
    rest = q.shape[2] - lead
    if rest > 0:
        n = rest // GDN_CHUNK

        def blocks(arr):
            arr = arr[:, :, lead:]
            return jnp.moveaxis(arr.reshape(arr.shape[:2] + (n, GDN_CHUNK) + arr.shape[3:]), 2, 0)

        def step(state, xs):
            return gdn_chunk(state, *xs)

        s, o = lax.scan(step, s, (blocks(q), blocks(k), blocks(v), blocks(beta), blocks(g)))
        o = jnp.moveaxis(o, 0, 2)
        outs.append(o.reshape(o.shape[:2] + (rest, o.shape[-1])))
    return s, jnp.concatenate(outs, axis=2)


def gdn_mixer(hn, conv_buf, s0, lead, w_in, conv_w, a_log, dt_bias, norm_g, w_out):
    b, t, _ = hn.shape
    f32 = jnp.float32
    proj = hn @ w_in
    qkv, new_buf = causal_conv(conv_buf, proj[..., :GDN_CONV_DIM], conv_w)
    z = proj[..., GDN_CONV_DIM:GDN_CONV_DIM + GDN_V_DIM].reshape(b, t, GDN_V_HEADS, GDN_HEAD_V)
    b_raw = proj[..., GDN_CONV_DIM + GDN_V_DIM:GDN_CONV_DIM + GDN_V_DIM + GDN_V_HEADS]
    a_raw = proj[..., GDN_CONV_DIM + GDN_V_DIM + GDN_V_HEADS:]
    q = qkv[..., :GDN_QK_DIM].reshape(b, t, GDN_QK_HEADS, GDN_HEAD_K).astype(f32)
    k = qkv[..., GDN_QK_DIM:2 * GDN_QK_DIM].reshape(b, t, GDN_QK_HEADS, GDN_HEAD_K).astype(f32)
    v = qkv[..., 2 * GDN_QK_DIM:].reshape(b, t, GDN_V_HEADS, GDN_HEAD_V).astype(f32)
    q = jnp.repeat(l2norm(q) * GDN_HEAD_K ** -0.5, GDN_GROUP, axis=2)
    k = jnp.repeat(l2norm(k), GDN_GROUP, axis=2)
    beta = jax.nn.sigmoid(b_raw.astype(f32))
    g = -jnp.exp(a_log.astype(f32)) * jax.nn.softplus(a_raw.astype(f32) + dt_bias.astype(f32))
    tr = lambda arr: jnp.swapaxes(arr, 1, 2)
    s_new, o = gdn_recurrence(s0.astype(f32), tr(q), tr(k), tr(v), tr(beta), tr(g), lead)
    o = rmsnorm(tr(o), norm_g) * jax.nn.silu(z.astype(f32))
    out = o.reshape(b, t, GDN_V_DIM).astype(hn.dtype) @ w_out
    return out, new_buf, s_new.astype(s0.dtype)


def diff_attn_prompt(q, k, v, lam, rel_bias):
    b, s = q.shape[:2]
    n_keys = k.shape[1]
    n_blocks = s // Q_BLOCK
    q_blocks = jnp.moveaxis(q.reshape(b, n_blocks, Q_BLOCK, N_MAPS, DIFF_HEAD), 1, 0)
    k_pos = jnp.arange(n_keys)
    kf = k.astype(jnp.float32)
    vf = v.astype(jnp.float32)

    def block(args):
        qi, i = args
        q_pos = N_META + i * Q_BLOCK + jnp.arange(Q_BLOCK)
        dist = q_pos[:, None] - k_pos[None, :]
        bias = jnp.moveaxis(rel_bias[t5_bucket(dist)].astype(jnp.float32), -1, 0)
        logits = jnp.einsum('bqmd,bkmd->bmqk', qi.astype(jnp.float32) * ATTN_SCALE, kf) + bias
        probs = jax.nn.softmax(jnp.where(dist >= 0, logits, -jnp.inf), axis=-1)
        probs = probs.reshape(b, DIFF_HEADS, 2, Q_BLOCK, n_keys)
        weights = probs[:, :, 0] - lam * probs[:, :, 1]
        return jnp.einsum('bhqk,bkhe->bqhe', weights, vf)

    o = lax.map(block, (q_blocks, jnp.arange(n_blocks)))
    return jnp.moveaxis(o, 0, 1).reshape(b, s, DIFF_HEADS, 2 * DIFF_HEAD)


def online_update(carry, s, v_blk):
    m, l, acc = carry
    m_new = jnp.maximum(m, jnp.max(s, axis=-1))
    corr = jnp.exp(m - m_new)
    p = jnp.exp(s - m_new[..., None])
    l = l * corr + jnp.sum(p, axis=-1)
    acc = acc * corr[..., None] + jnp.einsum('bhcqk,bkhe->bhcqe', p, v_blk.astype(jnp.float32))
    return (m_new, l, acc)


def diff_attn_sample(q, k_new, v_new, cache_k, cache_v, page_table, lam, rel_bias):
    db, t = q.shape[:2]
    q_pos = PAST_LEN + jnp.arange(t)
    qf = q.astype(jnp.float32) * ATTN_SCALE

    def logits(k_blk, k_pos):
        dist = q_pos[:, None] - k_pos[None, :]
        bias = jnp.moveaxis(rel_bias[t5_bucket(dist)].astype(jnp.float32), -1, 0)
        s = jnp.einsum('bqmd,bkmd->bmqk', qf, k_blk.astype(jnp.float32)) + bias
        s = jnp.where(dist >= 0, s, -jnp.inf)
        return s.reshape(db, DIFF_HEADS, 2, t, k_pos.shape[0])

    def page_step(carry, xs):
        phys, p = xs
        k_pos = p * PAGE_SIZE + jnp.arange(PAGE_SIZE)
        return online_update(carry, logits(cache_k[phys], k_pos), cache_v[phys]), None

    init = (jnp.full((db, DIFF_HEADS, 2, t), -jnp.inf, jnp.float32),
            jnp.zeros((db, DIFF_HEADS, 2, t), jnp.float32),
            jnp.zeros((db, DIFF_HEADS, 2, t, 2 * DIFF_HEAD), jnp.float32))
    n_pages = page_table.shape[1]
    carry, _ = lax.scan(page_step, init, (page_table.T, jnp.arange(n_pages)))
    m, l, acc = online_update(carry, logits(k_new, q_pos), v_new)
    o = acc / l[..., None]
    o = o[:, :, 0] - lam * o[:, :, 1]
    return jnp.swapaxes(o, 1, 2)


def diff_mixer(hn, attend, k_sh, v_sh, layer_idx, w_q, lam_vecs, subln, w_out):
    b, t, _ = hn.shape
    q = (hn @ w_q).reshape(b, t, N_MAPS, DIFF_HEAD)
    lam_init = 0.8 - 0.6 * math.exp(-0.3 * layer_idx)
    lv = lam_vecs.astype(jnp.float32)
    lam = jnp.exp(jnp.sum(lv[0] * lv[1])) - jnp.exp(jnp.sum(lv[2] * lv[3])) + lam_init
    o = attend(q, k_sh, v_sh, lam)
    o = rmsnorm(o, subln) * (1.0 - lam_init)
    return o.reshape(b, t, DIFF_V_DIM).astype(hn.dtype) @ w_out


def run_trunk(h, conv_state, ssm_state, lead, n_drop, attend, ln_gain, ffn_w_in, ffn_w_out,
              gdn_w_in, gdn_conv, gdn_a_log, gdn_dt_bias, gdn_norm, gdn_w_out, kv_norm, w_kv,
              diff_w_q, diff_lambda, diff_subln, diff_w_out, final_norm):
    new_conv, new_ssm = [], []
    k_sh = v_sh = None
    for i in range(DEPTH):
        h = h + 0.5 * swiglu(rmsnorm(h, ln_gain[i, 0]), ffn_w_in[i, 0], ffn_w_out[i, 0])
        hn = rmsnorm(h, ln_gain[i, 1])
        if i < N_A:
            mix, cb, st = gdn_mixer(hn, conv_state[i], ssm_state[i], lead, gdn_w_in[i], gdn_conv[i],
                                    gdn_a_log[i], gdn_dt_bias[i], gdn_norm[i], gdn_w_out[i])
            new_conv.append(cb)
            new_ssm.append(st)
        else:
            j = i - N_A
            mix = diff_mixer(hn, attend, k_sh, v_sh, i, diff_w_q[j], diff_lambda[j], diff_subln[j], diff_w_out[j])
        h = h + mix.astype(h.dtype)
        h = h + 0.5 * swiglu(rmsnorm(h, ln_gain[i, 2]), ffn_w_in[i, 1], ffn_w_out[i, 1])
        if i == N_A - 1:
            b, t, _ = h.shape
            kv = rmsnorm(h, kv_norm) @ w_kv
            k_sh = kv[..., :DIFF_QK_DIM].reshape(b, t, N_MAPS, DIFF_HEAD)
            v_sh = kv[..., DIFF_QK_DIM:].reshape(b, t, DIFF_HEADS, 2 * DIFF_HEAD)
            h = h[:, n_drop:]
    return rmsnorm(h, final_norm), k_sh, v_sh, jnp.stack(new_conv), jnp.stack(new_ssm)


def setup_inputs(seed: int = 0) -> dict:
    key = jax.random.key(seed)
    ks = jax.random.split(key, 28)
    f32 = jnp.float32
    nrm = lambda k, shape, scale: jax.random.normal(k, shape, f32) * scale
    gain = lambda k, shape: 1.0 + 0.02 * jax.random.normal(k, shape, f32)
    n_pages = PAST_LEN // PAGE_SIZE
    n_used = DEC_BATCH * n_pages
    n_phys = n_used + (n_used + 3) // 4
    page_table = jax.random.permutation(ks[6], n_phys)[:n_used].reshape(DEC_BATCH, n_pages).astype(jnp.int32)
    dt = jnp.exp(jax.random.uniform(ks[14], (N_A, GDN_V_HEADS), f32, math.log(1e-3), math.log(1e-1)))
    return {
        "x_prompt": nrm(ks[0], (BATCH, SEQ, D_MODEL), 1.0),
        "x_sample": nrm(ks[1], (DEC_BATCH, DEC_SEQ, D_MODEL), 1.0),
        "cache_k": nrm(ks[2], (n_phys, PAGE_SIZE, N_MAPS, DIFF_HEAD), 1.0),
        "cache_v": nrm(ks[3], (n_phys, PAGE_SIZE, DIFF_HEADS, 2 * DIFF_HEAD), 1.0),
        "state_conv": nrm(ks[4], (N_A, DEC_BATCH, GDN_CONV_W - 1, GDN_CONV_DIM), 1.0),
        "state_ssm": nrm(ks[5], (N_A, DEC_BATCH, GDN_V_HEADS, GDN_HEAD_K, GDN_HEAD_V), 0.5),
        "page_table": page_table,
        "meta_tokens": nrm(ks[7], (N_META, D_MODEL), 1.0),
        "ln_gain": gain(ks[8], (DEPTH, 3, D_MODEL)),
        "ffn_w_in": nrm(ks[9], (DEPTH, 2, D_MODEL, 2 * D_FF), D_MODEL ** -0.5),
        "ffn_w_out": nrm(ks[10], (DEPTH, 2, D_FF, D_MODEL), D_FF ** -0.5),
        "gdn_w_in": nrm(ks[11], (N_A, D_MODEL, GDN_PROJ), D_MODEL ** -0.5),
        "gdn_conv": nrm(ks[12], (N_A, GDN_CONV_W, GDN_CONV_DIM), GDN_CONV_W ** -0.5),
        "gdn_a_log": jnp.log(jax.random.uniform(ks[13], (N_A, GDN_V_HEADS), f32, 1.0, 16.0)),
        "gdn_dt_bias": dt + jnp.log(-jnp.expm1(-dt)),
        "gdn_norm": gain(ks[15], (N_A, GDN_HEAD_V)),
        "gdn_w_out": nrm(ks[16], (N_A, GDN_V_DIM, D_MODEL), GDN_V_DIM ** -0.5),
        "kv_norm": gain(ks[17], (D_MODEL,)),
        "w_kv": nrm(ks[18], (D_MODEL, DIFF_QK_DIM + DIFF_V_DIM), D_MODEL ** -0.5),
        "diff_w_q": nrm(ks[19], (N_B, D_MODEL, DIFF_QK_DIM), D_MODEL ** -0.5),
        "diff_lambda": nrm(ks[20], (N_B, 4, DIFF_HEAD), 0.1),
        "diff_subln": gain(ks[21], (N_B, 2 * DIFF_HEAD)),
        "diff_w_out": nrm(ks[22], (N_B, DIFF_V_DIM, D_MODEL), DIFF_V_DIM ** -0.5),
        "rel_bias": nrm(ks[23], (NUM_BUCKETS, N_MAPS), 0.2),
        "final_norm": gain(ks[24], (D_MODEL,)),
    }


def reference(x_prompt, x_sample, cache_k, cache_v, state_conv, state_ssm, page_table,
              meta_tokens, ln_gain, ffn_w_in, ffn_w_out, gdn_w_in, gdn_conv, gdn_a_log, gdn_dt_bias,
              gdn_norm, gdn_w_out, kv_norm, w_kv, diff_w_q, diff_lambda, diff_subln, diff_w_out,
              rel_bias, final_norm):
    weights = (ln_gain, ffn_w_in, ffn_w_out, gdn_w_in, gdn_conv, gdn_a_log, gdn_dt_bias, gdn_norm,
               gdn_w_out, kv_norm, w_kv, diff_w_q, diff_lambda, diff_subln, diff_w_out, final_norm)
    b = x_prompt.shape[0]
    meta = jnp.broadcast_to(meta_tokens.astype(x_prompt.dtype)[None], (b, N_META, D_MODEL))
    h_prompt = jnp.concatenate([meta, x_prompt], axis=1)
    conv0 = jnp.zeros((N_A, b, GDN_CONV_W - 1, GDN_CONV_DIM), x_prompt.dtype)
    ssm0 = jnp.zeros((N_A, b, GDN_V_HEADS, GDN_HEAD_K, GDN_HEAD_V), x_prompt.dtype)
    prompt_attend = lambda q, k, v, lam: diff_attn_prompt(q, k, v, lam, rel_bias)
    sample_attend = lambda q, k, v, lam: diff_attn_sample(q, k, v, cache_k, cache_v, page_table, lam, rel_bias)
    y_prompt, k_prompt, v_prompt, conv_prompt, ssm_prompt = run_trunk(
        h_prompt, conv0, ssm0, N_META, N_META, prompt_attend, *weights)
    y_sample, k_sample, v_sample, conv_sample, ssm_sample = run_trunk(
        x_sample, state_conv, state_ssm, x_sample.shape[1], 0, sample_attend, *weights)
    return (y_prompt, y_sample, k_prompt, v_prompt, conv_prompt, ssm_prompt,
            k_sample, v_sample, conv_sample, ssm_sample)
```

```python
import functools
import math

import jax
import jax.numpy as jnp
from jax import lax
from jax.experimental import pallas as pl
from jax.experimental.pallas import tpu as pltpu

F32 = jnp.float32
BF16 = jnp.bfloat16

EPS = 1e-6
L2_EPS = 1e-6
N_META = 16
GDN_HEAD = 128
DIFF_HEAD = 128
NUM_BUCKETS = 32
MAX_DISTANCE = 128
NEG_BIG = -1e30
VMEM_LIMIT = 56 * 1024 * 1024


def _params(*sem):
    return pltpu.CompilerParams(dimension_semantics=sem, vmem_limit_bytes=VMEM_LIMIT)


def _bdot(a, b):
    return jnp.dot(a.astype(BF16), b.astype(BF16), preferred_element_type=F32)


def _bdot_nt(a, b):
    return lax.dot_general(a.astype(BF16), b.astype(BF16), (((1,), (1,)), ((), ())),
                           preferred_element_type=F32)


def _bdot_tn(a, b):
    return lax.dot_general(a.astype(BF16), b.astype(BF16), (((0,), (0,)), ((), ())),
                           preferred_element_type=F32)


def _hdot(a, b):
    return jnp.dot(a, b, precision=lax.Precision.HIGHEST, preferred_element_type=F32)


def _split3(x):
    hi = x.astype(BF16)
    r1 = x - hi.astype(F32)
    mid = r1.astype(BF16)
    lo = (r1 - mid.astype(F32)).astype(BF16)
    return hi, mid, lo


def _exact_left_dot(mask01, x):
    m = mask01.astype(BF16)
    hi, mid, lo = _split3(x)
    d = functools.partial(jnp.dot, preferred_element_type=F32)
    return d(m, hi) + d(m, mid) + d(m, lo)


def _silu(x):
    return x * jax.nn.sigmoid(x)


def _rms_scale(x):
    return lax.rsqrt(jnp.mean(x * x, axis=-1, keepdims=True) + EPS)


def _rmsnorm_kernel(x_ref, g_ref, o_ref):
    x = x_ref[...].astype(F32)
    o_ref[...] = (x * _rms_scale(x) * g_ref[...]).astype(o_ref.dtype)


def rmsnorm(x, gain, tm):
    m, d = x.shape
    return pl.pallas_call(
        _rmsnorm_kernel,
        grid=(pl.cdiv(m, tm),),
        in_specs=[pl.BlockSpec((tm, d), lambda i: (i, 0)), pl.BlockSpec((1, d), lambda i: (0, 0))],
        out_specs=pl.BlockSpec((tm, d), lambda i: (i, 0)),
        out_shape=jax.ShapeDtypeStruct((m, d), BF16),
        compiler_params=_params("parallel"),
        name="rmsnorm",
    )(x, gain.reshape(1, d))


def _ffn_in_kernel(x_ref, wg_ref, wu_ref, o_ref):
    x = x_ref[...]
    g = jnp.dot(x, wg_ref[...].astype(BF16), preferred_element_type=F32)
    u = jnp.dot(x, wu_ref[...].astype(BF16), preferred_element_type=F32)
    o_ref[...] = (_silu(g) * u).astype(o_ref.dtype)


def ffn_in(x, w_in, layer, which, tm, tn):
    m, d = x.shape
    d_ff = w_in.shape[-1] // 2
    ncol = d_ff // tn
    assert d_ff % tn == 0
    wspec = lambda off: pl.BlockSpec((None, None, d, tn), lambda j, i: (layer, which, 0, j + off))
    return pl.pallas_call(
        _ffn_in_kernel,
        grid=(ncol, pl.cdiv(m, tm)),
        in_specs=[pl.BlockSpec((tm, d), lambda j, i: (i, 0)), wspec(0), wspec(ncol)],
        out_specs=pl.BlockSpec((tm, tn), lambda j, i: (i, j)),
        out_shape=jax.ShapeDtypeStruct((m, d_ff), BF16),
        compiler_params=_params("parallel", "parallel"),
        name="ffn_in",
    )(x, w_in, w_in)


def _mm_res_norm_kernel(a_ref, w_ref, h_ref, g_ref, oh_ref, *on_refs, scale, nk):
    k = pl.program_id(1)

    @pl.when(k == 0)
    def _():
        oh_ref[...] = h_ref[...]

    oh_ref[...] += scale * _bdot(a_ref[...], w_ref[...])

    @pl.when(k == nk - 1)
    def _():
        h = oh_ref[...]
        hs = h * _rms_scale(h)
        for n, on in enumerate(on_refs):
            on[...] = (hs * g_ref[n:n + 1, :]).astype(on.dtype)


def mm_res_norm(a, w, w_idx, h, gains, out_dtypes, scale, tm, tk):
    m, kdim = a.shape
    d = h.shape[-1]
    nk = kdim // tk
    assert kdim % tk == 0
    n_norm = gains.shape[0]
    lead = (None,) * len(w_idx)
    row = lambda i, k: (i, 0)
    outs = pl.pallas_call(
        functools.partial(_mm_res_norm_kernel, scale=scale, nk=nk),
        grid=(pl.cdiv(m, tm), nk),
        in_specs=[
            pl.BlockSpec((tm, tk), lambda i, k: (i, k)),
            pl.BlockSpec(lead + (tk, d), lambda i, k: tuple(w_idx) + (k, 0)),
            pl.BlockSpec((tm, d), row),
            pl.BlockSpec((n_norm, d), lambda i, k: (0, 0)),
        ],
        out_specs=[pl.BlockSpec((tm, d), row)] * (1 + n_norm),
        out_shape=[jax.ShapeDtypeStruct((m, d), F32)]
        + [jax.ShapeDtypeStruct((m, d), dt) for dt in out_dtypes],
        compiler_params=_params("parallel", "arbitrary"),
        name="mm_res_norm",
    )(a, w, h, gains)
    return outs[0], outs[1:]


def _mm_kernel(x_ref, w_ref, o_ref):
    o_ref[...] = _bdot(x_ref[...], w_ref[...]).astype(o_ref.dtype)


def mm(x, w, w_idx, col0, ncols, tm, tn):
    m, kdim = x.shape
    assert col0 % tn == 0 and ncols % tn == 0
    lead = (None,) * len(w_idx)
    c0 = col0 // tn
    return pl.pallas_call(
        _mm_kernel,
        grid=(ncols // tn, pl.cdiv(m, tm)),
        in_specs=[pl.BlockSpec((tm, kdim), lambda j, i: (i, 0)),
                  pl.BlockSpec(lead + (kdim, tn), lambda j, i: tuple(w_idx) + (0, c0 + j))],
        out_specs=pl.BlockSpec((tm, tn), lambda j, i: (i, j)),
        out_shape=jax.ShapeDtypeStruct((m, ncols), F32),
        compiler_params=_params("parallel", "parallel"),
        name="mm",
    )(x, w)


def _gdn_ba_kernel(x_ref, w_ref, alog_ref, dtb_ref, o_ref):
    r = _bdot(x_ref[...], w_ref[...])
    nh = r.shape[1] // 2
    lane = lax.broadcasted_iota(jnp.int32, r.shape, 1)
    xa = r + dtb_ref[...]
    softplus = jnp.maximum(xa, 0.0) + jnp.log1p(jnp.exp(-jnp.abs(xa)))
    g = -jnp.exp(alog_ref[...]) * softplus
    o_ref[...] = jnp.where(lane < nh, jax.nn.sigmoid(r), g)


def gdn_ba(x, w_ba, a_log, dt_bias, tm):
    m, d = x.shape
    nh = a_log.shape[0]
    pad = jnp.zeros((nh,), F32)
    alog = jnp.concatenate([pad, a_log.astype(F32)]).reshape(1, 2 * nh)
    dtb = jnp.concatenate([pad, dt_bias.astype(F32)]).reshape(1, 2 * nh)
    return pl.pallas_call(
        _gdn_ba_kernel,
        grid=(pl.cdiv(m, tm),),
        in_specs=[pl.BlockSpec((tm, d), lambda i: (i, 0)), pl.BlockSpec((d, 2 * nh), lambda i: (0, 0)),
                  pl.BlockSpec((1, 2 * nh), lambda i: (0, 0)), pl.BlockSpec((1, 2 * nh), lambda i: (0, 0))],
        out_specs=pl.BlockSpec((tm, 2 * nh), lambda i: (i, 0)),
        out_shape=jax.ShapeDtypeStruct((m, 2 * nh), F32),
        compiler_params=_params("parallel"),
        name="gdn_ba",
    )(x, w_ba, alog, dtb)


def _l2norm(x):
    return x * lax.rsqrt(jnp.sum(x * x, axis=-1, keepdims=True) + L2_EPS)


def _col(x, idx):
    lane = lax.broadcasted_iota(jnp.int32, x.shape, 1)
    return jnp.sum(jnp.where(lane == idx, x, 0.0), axis=-1, keepdims=True)


def _row_of(col):
    r = col.shape[0]
    return jnp.broadcast_to(col, (r, max(r, 128))).T[:r, :]


def _unit_lower_inverse(a, n_sq):
    r = a.shape[0]
    eye = (lax.broadcasted_iota(jnp.int32, (r, r), 0) == lax.broadcasted_iota(jnp.int32, (r, r), 1)).astype(F32)
    p = -a
    t = eye + p
    for _ in range(n_sq):
        p = _hdot(p, p)
        t = t + _hdot(t, p)
    return t


def _delta_wy(kk, qk, k, q, v, beta, gc, tril, strict, n_sq):
    dk = k.shape[1]
    decay = jnp.exp(jnp.where(tril, gc - _row_of(gc), -jnp.inf))
    a = jnp.where(strict, beta * kk * decay, 0.0)
    t = _unit_lower_inverse(a, n_sq)
    egc = jnp.exp(gc)
    kb = k * beta
    sol = _hdot(t, jnp.concatenate([v * beta, kb * egc], axis=1))
    return sol[:, :dk], sol[:, dk:], q * egc, qk * decay


def _gdn_prompt_kernel(q_ref, k_ref, v_ref, z_ref, bg_ref, wq_ref, wk_ref, wv_ref, ng_ref,
                       o_ref, s_out_ref, qwin, kwin, vwin, s_ref, *, seq_len, chunk, g_heads, n_sq):
    h = pl.program_id(1)
    c = pl.program_id(2)
    nc = pl.num_programs(2)
    dk = GDN_HEAD
    lead = 8

    @pl.when(c == 0)
    def _():
        s_ref[...] = jnp.zeros_like(s_ref)
        for win in (qwin, kwin, vwin):
            win[pl.ds(0, lead), :] = jnp.zeros((lead, win.shape[1]), F32)

    row = lax.broadcasted_iota(jnp.int32, (chunk, 1), 0)
    valid = (c * chunk + row) < seq_len

    def conv(win, x_ref, w_ref):
        win[pl.ds(lead, chunk), :] = x_ref[0]
        acc = win[pl.ds(lead - 3, chunk), :] * w_ref[0:1, :]
        for i in range(1, 4):
            acc = acc + win[pl.ds(lead - 3 + i, chunk), :] * w_ref[i:i + 1, :]
        win[pl.ds(lead - 3, 3), :] = win[pl.ds(lead + chunk - 3, 3), :]
        return jnp.where(valid, _silu(acc), 0.0)

    qc = conv(qwin, q_ref, wq_ref)
    kc = conv(kwin, k_ref, wk_ref)
    vc = conv(vwin, v_ref, wv_ref)
    bg = jnp.where(valid, bg_ref[0], 0.0)
    nh = bg.shape[1] // 2

    ri = lax.broadcasted_iota(jnp.int32, (chunk, chunk), 0)
    ci = lax.broadcasted_iota(jnp.int32, (chunk, chunk), 1)
    tril = ri >= ci
    strict = ri > ci
    gc_all = _exact_left_dot(tril, bg)

    outs = []
    for gi in range(g_heads):
        q = _l2norm(qc[:, gi * dk:(gi + 1) * dk]) * (dk ** -0.5)
        k = _l2norm(kc[:, gi * dk:(gi + 1) * dk])
        kk = _bdot_nt(k, k)
        qk = _bdot_nt(q, k)
        for e in range(2):
            hv = 2 * (h * g_heads + gi) + e
            sv = 2 * gi + e
            v = vc[:, sv * dk:(sv + 1) * dk]
            beta = _col(bg, hv)
            gc = _col(gc_all, nh + hv)
            u, w, qg, qkd = _delta_wy(kk, qk, k, q, v, beta, gc, tril, strict, n_sq)
            s = s_ref[sv]
            v_new = u - _bdot(w, s)
            o = _bdot(qg, s) + _bdot(qkd, v_new)
            g_last = gc[chunk - 1:chunk, :]
            s_ref[sv] = s * jnp.exp(g_last) + _bdot_tn(k * jnp.exp(g_last - gc), v_new)
            z = z_ref[0, :, sv * dk:(sv + 1) * dk]
            outs.append(o * _rms_scale(o) * ng_ref[...] * _silu(z))
    o_ref[0] = jnp.concatenate(outs, axis=1).astype(o_ref.dtype)

    @pl.when(c == nc - 1)
    def _():
        s_out_ref[0] = s_ref[...]


def gdn_prompt(raw, bg, conv_w, norm_g, chunk, g_heads):
    b, t, _ = raw.shape
    nh = bg.shape[-1] // 2
    dk = GDN_HEAD
    n_qk = nh // 2
    qk_dim = n_qk * dk
    gq, gv = g_heads * dk, 2 * g_heads * dk
    n_hblk = n_qk // g_heads
    nc = pl.cdiv(t, chunk)
    n_sq = int(math.log2(chunk)) - 1
    assert 2 ** (n_sq + 1) == chunk and n_qk % g_heads == 0
    blk = lambda width, off: pl.BlockSpec((1, chunk, width), lambda bi, h, c: (bi, c, off + h))
    wblk = lambda width, off: pl.BlockSpec((4, width), lambda bi, h, c: (0, off + h))
    kernel = functools.partial(_gdn_prompt_kernel, seq_len=t, chunk=chunk, g_heads=g_heads, n_sq=n_sq)
    return pl.pallas_call(
        kernel,
        grid=(b, n_hblk, nc),
        in_specs=[
            blk(gq, 0), blk(gq, n_hblk), blk(gv, n_hblk), blk(gv, 2 * n_hblk),
            pl.BlockSpec((1, chunk, 2 * nh), lambda bi, h, c: (bi, c, 0)),
            wblk(gq, 0), wblk(gq, n_hblk), wblk(gv, n_hblk),
            pl.BlockSpec((1, dk), lambda bi, h, c: (0, 0)),
        ],
        out_specs=[
            pl.BlockSpec((1, chunk, gv), lambda bi, h, c: (bi, c, h)),
            pl.BlockSpec((1, 2 * g_heads, dk, dk), lambda bi, h, c: (bi, h, 0, 0)),
        ],
        out_shape=[jax.ShapeDtypeStruct((b, t, 2 * qk_dim), BF16),
                   jax.ShapeDtypeStruct((b, nh, dk, dk), F32)],
        scratch_shapes=[pltpu.VMEM((8 + chunk, gq), F32), pltpu.VMEM((8 + chunk, gq), F32),
                        pltpu.VMEM((8 + chunk, gv), F32), pltpu.VMEM((2 * g_heads, dk, dk), F32)],
        compiler_params=_params("parallel", "parallel", "arbitrary"),
        name="gdn_prompt",
    )(raw, raw, raw, raw, bg, conv_w, conv_w, conv_w, norm_g.reshape(1, dk))


def _gdn_sample_kernel(q_ref, k_ref, v_ref, z_ref, bg_ref, cq_ref, ck_ref, cv_ref, wq_ref, wk_ref, wv_ref,
                       ng_ref, s_in_ref, o_ref, s_out_ref, qwin, kwin, vwin, *, n_sq):
    h = pl.program_id(0)
    tb, t, dk = q_ref.shape
    r = tb * t
    lead = 8

    def conv(win, x_ref, c_ref, w_ref):
        win[:, pl.ds(lead - 3, 3), :] = c_ref[...]
        win[:, pl.ds(lead, t), :] = x_ref[...]
        acc = win[:, pl.ds(lead - 3, t), :] * w_ref[0:1, :]
        for i in range(1, 4):
            acc = acc + win[:, pl.ds(lead - 3 + i, t), :] * w_ref[i:i + 1, :]
        return _silu(acc).reshape(r, x_ref.shape[-1])

    qc = conv(qwin, q_ref, cq_ref, wq_ref)
    kc = conv(kwin, k_ref, ck_ref, wk_ref)
    vc = conv(vwin, v_ref, cv_ref, wv_ref)
    bg = bg_ref[...].reshape(r, bg_ref.shape[-1])
    nh = bg.shape[1] // 2

    ri = lax.broadcasted_iota(jnp.int32, (r, r), 0)
    ci = lax.broadcasted_iota(jnp.int32, (r, r), 1)
    same = (ri // t) == (ci // t)
    tril = same & (ri >= ci)
    strict = same & (ri > ci)
    gc_all = _exact_left_dot(tril, bg)
    g_sum = jnp.sum(bg_ref[...], axis=1, keepdims=True)

    q = _l2norm(qc) * (dk ** -0.5)
    k = _l2norm(kc)
    kk = _bdot_nt(k, k)
    qk = _bdot_nt(q, k)
    bdims = (((2,), (1,)), ((0,), (0,)))
    for e in range(2):
        hv = 2 * h + e
        v = vc[:, e * dk:(e + 1) * dk]
        beta = _col(bg, hv)
        gc = _col(gc_all, nh + hv)
        u, w, qg, qkd = _delta_wy(kk, qk, k, q, v, beta, gc, tril, strict, n_sq)
        s = s_in_ref[:, e]
        wq3 = jnp.concatenate([w.reshape(tb, t, dk), qg.reshape(tb, t, dk)], axis=1).astype(BF16)
        ws_qs = lax.dot_general(wq3, s.astype(BF16), bdims, preferred_element_type=F32)
        v_new = u - ws_qs[:, :t].reshape(r, dk)
        o = ws_qs[:, t:].reshape(r, dk) + _bdot(qkd, v_new)
        lane3 = lax.broadcasted_iota(jnp.int32, g_sum.shape, 2)
        g_last = jnp.sum(jnp.where(lane3 == nh + hv, g_sum, 0.0), axis=-1, keepdims=True)
        kd = (k.reshape(tb, t, dk) * jnp.exp(g_last - gc.reshape(tb, t, 1))).astype(BF16)
        upd = lax.dot_general(kd, v_new.reshape(tb, t, dk).astype(BF16), (((1,), (1,)), ((0,), (0,))),
                              preferred_element_type=F32)
        s_out_ref[:, e] = s * jnp.exp(g_last) + upd
        z = z_ref[:, :, e * dk:(e + 1) * dk].reshape(r, dk)
        o_ref[:, :, e * dk:(e + 1) * dk] = (o * _rms_scale(o) * ng_ref[...] * _silu(z)).reshape(tb, t, dk)


def gdn_sample(raw, bg, conv_state, ssm_state, conv_w, norm_g, tb):
    b, t, _ = raw.shape
    nh = bg.shape[-1] // 2
    dk = GDN_HEAD
    n_qk = nh // 2
    n_sq = int(math.log2(t)) - 1
    assert 2 ** (n_sq + 1) == t and b % tb == 0 and t == 8
    blk = lambda width, off: pl.BlockSpec((tb, t, width), lambda h, i: (i, 0, off + h))
    cblk = lambda width, off: pl.BlockSpec((tb, 3, width), lambda h, i: (i, 0, off + h))
    wblk = lambda width, off: pl.BlockSpec((4, width), lambda h, i: (0, off + h))
    sblk = pl.BlockSpec((tb, 2, dk, dk), lambda h, i: (i, h, 0, 0))
    return pl.pallas_call(
        functools.partial(_gdn_sample_kernel, n_sq=n_sq),
        grid=(n_qk, b // tb),
        in_specs=[
            blk(dk, 0), blk(dk, n_qk), blk(2 * dk, n_qk), blk(2 * dk, 2 * n_qk),
            pl.BlockSpec((tb, t, 2 * nh), lambda h, i: (i, 0, 0)),
            cblk(dk, 0), cblk(dk, n_qk), cblk(2 * dk, n_qk),
            wblk(dk, 0), wblk(dk, n_qk), wblk(2 * dk, n_qk),
            pl.BlockSpec((1, dk), lambda h, i: (0, 0)),
            sblk,
        ],
        out_specs=[pl.BlockSpec((tb, t, 2 * dk), lambda h, i: (i, 0, h)), sblk],
        out_shape=[jax.ShapeDtypeStruct((b, t, nh * dk), F32),
                   jax.ShapeDtypeStruct(ssm_state.shape, F32)],
        scratch_shapes=[pltpu.VMEM((tb, 8 + t, dk), F32), pltpu.VMEM((tb, 8 + t, dk), F32),
                        pltpu.VMEM((tb, 8 + t, 2 * dk), F32)],
        compiler_params=_params("parallel", "parallel"),
        name="gdn_sample",
    )(raw, raw, raw, raw, bg, conv_state, conv_state, conv_state, conv_w, conv_w, conv_w,
      norm_g.reshape(1, dk), ssm_state)


def _t5_bucket(dist):
    max_exact = NUM_BUCKETS // 2
    n = jnp.maximum(dist, 0)
    log_ratio = jnp.log(jnp.maximum(n, 1).astype(F32) * (1.0 / max_exact)) * (1.0 / math.log(MAX_DISTANCE / max_exact))
    large = jnp.minimum(max_exact + (log_ratio * (NUM_BUCKETS - max_exact)).astype(jnp.int32), NUM_BUCKETS - 1)
    return jnp.where(n < max_exact, n, large)


def _prompt_bias_kernel(rb_ref, o_ref, *, tq, tk, n_meta):
    ty = pl.program_id(0)
    m = pl.program_id(1)
    ii = lax.broadcasted_iota(jnp.int32, (tq, tk), 0)
    jj = lax.broadcasted_iota(jnp.int32, (tq, tk), 1)
    dist = n_meta + (ty - 1) * tk + ii - jj
    bucket = _t5_bucket(dist)
    tile = jnp.full((tq, tk), NEG_BIG, F32)
    for bkt in range(NUM_BUCKETS):
        tile = jnp.where((bucket == bkt) & (dist >= 0), rb_ref[bkt, m], tile)
    o_ref[0, 0] = tile


def prompt_bias_tiles(rel_bias, tq, tk, n_meta):
    n_maps = rel_bias.shape[1]
    return pl.pallas_call(
        functools.partial(_prompt_bias_kernel, tq=tq, tk=tk, n_meta=n_meta),
        grid=(3, n_maps),
        in_specs=[pl.BlockSpec(memory_space=pltpu.SMEM)],
        out_specs=pl.BlockSpec((1, 1, tq, tk), lambda ty, m: (ty, m, 0, 0)),
        out_shape=jax.ShapeDtypeStruct((3, n_maps, tq, tk), F32),
        compiler_params=_params("parallel", "parallel"),
        name="prompt_bias",
    )(rel_bias.astype(F32))


def _lambda_of(lv_ref, lam_init):
    lv = lv_ref[...].astype(F32)
    a = jnp.sum(lv[0:1] * lv[1:2], axis=-1, keepdims=True)
    b = jnp.sum(lv[2:3] * lv[3:4], axis=-1, keepdims=True)
    return jnp.exp(a) - jnp.exp(b) + lam_init


def _attn_prompt_kernel(qi_ref, kj_ref, q_ref, k_ref, v_ref, bias_ref, rb_ref, lv_ref, sub_ref, o_ref,
                        m_ref, l_ref, acc_ref, *, n_keys, nk, tk, lam_init, scale):
    h = pl.program_id(1)
    step = pl.program_id(2)
    qi = qi_ref[step]
    kj = kj_ref[step]
    dh = DIFF_HEAD

    @pl.when(kj == 0)
    def _():
        m_ref[...] = jnp.full_like(m_ref, NEG_BIG)
        l_ref[...] = jnp.zeros_like(l_ref)
        acc_ref[...] = jnp.zeros_like(acc_ref)

    krow = lax.broadcasted_iota(jnp.int32, (tk, 1), 0)
    kvalid = (kj * tk + krow) < n_keys
    k = jnp.where(kvalid, k_ref[0], 0.0).astype(BF16)
    v = jnp.where(kvalid, v_ref[0], 0.0).astype(BF16)
    near = (qi - kj) <= 1
    for c in range(2):
        q = (q_ref[0, :, c * dh:(c + 1) * dh] * scale).astype(BF16)
        s = _bdot_nt(q, k[:, c * dh:(c + 1) * dh])
        far_bias = rb_ref[NUM_BUCKETS - 1, 2 * h + c]
        s = s + jnp.where(near, bias_ref[0, c], far_bias)
        m_old = m_ref[c]
        m_new = jnp.maximum(m_old, jnp.max(s, axis=-1, keepdims=True))
        corr = jnp.exp(m_old - m_new)
        p = jnp.exp(s - m_new)
        l_ref[c] = l_ref[c] * corr + jnp.sum(p, axis=-1, keepdims=True)
        acc_ref[c] = acc_ref[c] * corr + _bdot(p, v)
        m_ref[c] = m_new

    @pl.when(kj == jnp.minimum(qi + 1, nk - 1))
    def _():
        lam = _lambda_of(lv_ref, lam_init)
        o = acc_ref[0] / l_ref[0] - lam * (acc_ref[1] / l_ref[1])
        o_ref[0] = (o * _rms_scale(o) * sub_ref[...] * (1.0 - lam_init)).astype(o_ref.dtype)


def attn_prompt(q, k, v, bias_tiles, rel_bias, lam_vecs, subln, lam_init, tq, tk):
    b, s, d = q.shape
    n_keys = k.shape[1]
    n_heads = d // (2 * DIFF_HEAD)
    hw = 2 * DIFF_HEAD
    nq, nk = s // tq, pl.cdiv(n_keys, tk)
    assert s % tq == 0 and tq == tk and N_META <= tk
    assert N_META + 2 * tk - (tk - 1) >= MAX_DISTANCE
    pairs = [(qi, kj) for qi in range(nq) for kj in range(min(qi + 1, nk - 1) + 1)]
    qi_tab = jnp.asarray([p[0] for p in pairs], jnp.int32)
    kj_tab = jnp.asarray([p[1] for p in pairs], jnp.int32)

    qmap = lambda bi, h, st, qt, kt: (bi, qt[st], h)
    kvmap = lambda bi, h, st, qt, kt: (bi, kt[st], h)
    bias_map = lambda bi, h, st, qt, kt: (jnp.clip(qt[st] - kt[st] + 1, 0, 2), h, 0, 0)
    const2 = lambda bi, h, st, qt, kt: (0, 0)
    kernel = functools.partial(_attn_prompt_kernel, n_keys=n_keys, nk=nk, tk=tk, lam_init=lam_init,
                               scale=DIFF_HEAD ** -0.5)
    grid_spec = pltpu.PrefetchScalarGridSpec(
        num_scalar_prefetch=2,
        grid=(b, n_heads, len(pairs)),
        in_specs=[
            pl.BlockSpec((1, tq, hw), qmap),
            pl.BlockSpec((1, tk, hw), kvmap),
            pl.BlockSpec((1, tk, hw), kvmap),
            pl.BlockSpec((1, 2, tq, tk), bias_map),
            pl.BlockSpec(memory_space=pltpu.SMEM),
            pl.BlockSpec(lam_vecs.shape, const2),
            pl.BlockSpec((1, hw), const2),
        ],
        out_specs=pl.BlockSpec((1, tq, hw), qmap),
        scratch_shapes=[pltpu.VMEM((2, tq, 1), F32), pltpu.VMEM((2, tq, 1), F32), pltpu.VMEM((2, tq, hw), F32)],
    )
    return pl.pallas_call(
        kernel,
        grid_spec=grid_spec,
        out_shape=jax.ShapeDtypeStruct((b, s, d), BF16),
        compiler_params=_params("parallel", "parallel", "arbitrary"),
        name="attn_prompt",
    )(qi_tab, kj_tab, q, k, v, bias_tiles, rel_bias.astype(F32), lam_vecs, subln.reshape(1, hw))


def _sample_bias_kernel(rb_ref, o_ref, *, t, page, past_len):
    r = o_ref.shape[1]
    ri = lax.broadcasted_iota(jnp.int32, (r, page), 0)
    jj = lax.broadcasted_iota(jnp.int32, (r, page), 1)
    tq = ri % t
    far = rb_ref[:, NUM_BUCKETS - 1:NUM_BUCKETS]
    o_ref[0] = jnp.broadcast_to(far, (r, page))
    for ty, dist, ok in ((1, page + tq - jj, None), (2, tq - jj, (jj <= tq) & (jj < t))):
        bucket = _t5_bucket(dist)
        tile = jnp.full((r, page), NEG_BIG, F32)
        for bkt in range(NUM_BUCKETS):
            hit = (bucket == bkt) if ok is None else ((bucket == bkt) & ok)
            tile = jnp.where(hit, rb_ref[:, bkt:bkt + 1], tile)
        o_ref[ty] = tile
    del past_len


def sample_bias_tiles(rel_bias, t, page, past_len):
    n_maps = rel_bias.shape[1]
    assert past_len % page == 0 and past_len - page >= MAX_DISTANCE
    rb_rows = jnp.repeat(rel_bias.astype(F32).T, t, axis=0)
    return pl.pallas_call(
        functools.partial(_sample_bias_kernel, t=t, page=page, past_len=past_len),
        out_shape=jax.ShapeDtypeStruct((3, n_maps * t, page), F32),
        compiler_params=pltpu.CompilerParams(vmem_limit_bytes=VMEM_LIMIT),
        name="sample_bias",
    )(rb_rows)


def _attn_sample_kernel(pt_ref, q_ref, kc_ref, vc_ref, kn_ref, vn_ref, bias_ref, nbias_ref, lv_ref, sub_ref, o_ref,
                        qb_ref, m_ref, l_ref, acc_ref, kpad, vpad, *, lam_init, scale):
    del pt_ref
    p = pl.program_id(1)
    n_pages = pl.num_programs(1)
    t, d = q_ref.shape[1], q_ref.shape[2]
    dh = DIFF_HEAD
    n_maps = d // dh
    page = kc_ref.shape[0]

    @pl.when(p == 0)
    def _():
        q = q_ref[0] * scale
        lane_map = lax.broadcasted_iota(jnp.int32, (t, d), 1) // dh
        for m in range(n_maps):
            qb_ref[pl.ds(m * t, t), :] = jnp.where(lane_map == m, q, 0.0).astype(BF16)
        m_ref[...] = jnp.full_like(m_ref, NEG_BIG)
        l_ref[...] = jnp.zeros_like(l_ref)
        acc_ref[...] = jnp.zeros_like(acc_ref)

    def update(k, v, bias):
        s = _bdot_nt(qb_ref[...], k) + bias
        m_old = m_ref[...]
        m_new = jnp.maximum(m_old, jnp.max(s, axis=-1, keepdims=True))
        corr = jnp.exp(m_old - m_new)
        pr = jnp.exp(s - m_new)
        l_ref[...] = l_ref[...] * corr + jnp.sum(pr, axis=-1, keepdims=True)
        acc_ref[...] = acc_ref[...] * corr + _bdot(pr, v)
        m_ref[...] = m_new

    update(kc_ref[...], vc_ref[...], bias_ref[0])

    @pl.when(p == n_pages - 1)
    def _():
        kpad[...] = jnp.zeros_like(kpad)
        vpad[...] = jnp.zeros_like(vpad)
        kpad[pl.ds(0, t), :] = kn_ref[0]
        vpad[pl.ds(0, t), :] = vn_ref[0]
        update(kpad[...], vpad[...], nbias_ref[0])
        lam = _lambda_of(lv_ref, lam_init)
        hw = 2 * dh
        for h in range(n_maps // 2):
            r0 = 2 * h * t
            o0 = acc_ref[pl.ds(r0, t), pl.ds(h * hw, hw)] / l_ref[pl.ds(r0, t), :]
            o1 = acc_ref[pl.ds(r0 + t, t), pl.ds(h * hw, hw)] / l_ref[pl.ds(r0 + t, t), :]
            o = o0 - lam * o1
            o_ref[0, :, h * hw:(h + 1) * hw] = o * _rms_scale(o) * sub_ref[...] * (1.0 - lam_init)


def attn_sample(q, k_new, v_new, cache_k, cache_v, page_table, bias_tiles, lam_vecs, subln, lam_init):
    b, t, d = q.shape
    n_pages = page_table.shape[1]
    page = cache_k.shape[1]
    hw = 2 * DIFF_HEAD
    r = (d // DIFF_HEAD) * t
    assert r == bias_tiles.shape[1] and page == bias_tiles.shape[2]

    def cache_map(bi, p, pt):
        return (pt[bi, p], 0, 0)

    def bias_map(bi, p, pt):
        return (jnp.where(p == n_pages - 1, 1, 0), 0, 0)

    row = lambda bi, p, pt: (bi, 0, 0)
    grid_spec = pltpu.PrefetchScalarGridSpec(
        num_scalar_prefetch=1,
        grid=(b, n_pages),
        in_specs=[
            pl.BlockSpec((1, t, d), row),
            pl.BlockSpec((None, page, d), cache_map),
            pl.BlockSpec((None, page, d), cache_map),
            pl.BlockSpec((1, t, d), row),
            pl.BlockSpec((1, t, d), row),
            pl.BlockSpec((1, r, page), bias_map),
            pl.BlockSpec((1, r, page), lambda bi, p, pt: (2, 0, 0)),
            pl.BlockSpec(lam_vecs.shape, lambda bi, p, pt: (0, 0)),
            pl.BlockSpec((1, hw), lambda bi, p, pt: (0, 0)),
        ],
        out_specs=pl.BlockSpec((1, t, d), row),
        scratch_shapes=[pltpu.VMEM((r, d), BF16), pltpu.VMEM((r, 1), F32), pltpu.VMEM((r, 1), F32),
                        pltpu.VMEM((r, d), F32), pltpu.VMEM((page, d), F32), pltpu.VMEM((page, d), F32)],
    )
    return pl.pallas_call(
        functools.partial(_attn_sample_kernel, lam_init=lam_init, scale=DIFF_HEAD ** -0.5),
        grid_spec=grid_spec,
        out_shape=jax.ShapeDtypeStruct((b, t, d), F32),
        compiler_params=_params("parallel", "arbitrary"),
        name="attn_sample",
    )(page_table, q, cache_k, cache_v, k_new, v_new, bias_tiles, bias_tiles, lam_vecs, subln.reshape(1, hw))


TN = 512
TK = 512
ATTN_TILE = 256
GDN_CHUNK = 64
GDN_HEADS_PER_STEP = 4
GDN_SEQS_PER_STEP = 16


def _row_tile(m, target):
    best = 16
    for t in range(16, target + 1, 16):
        if m % t == 0:
            best = t
    assert m % best == 0
    return best


def _trunk(h, n_seq, n_drop, gdn_fn, attn_fn, kv_fn, ln_gain, ffn_w_in, ffn_w_out, gdn_w_out, kv_norm,
           diff_w_q, diff_w_out, final_norm, n_a, tm_target):
    depth = ln_gain.shape[0]
    d = h.shape[-1]
    tm = _row_tile(h.shape[0], tm_target)
    hn = rmsnorm(h, ln_gain[0, 0], tm)
    conv_states, ssm_states, kv = [], [], None
    y = None
    for i in range(depth):
        tm = _row_tile(h.shape[0], tm_target)
        act = ffn_in(hn, ffn_w_in, i, 0, tm, TN)
        h, (hn,) = mm_res_norm(act, ffn_w_out, (i, 0), h, ln_gain[i, 1][None], [BF16], 0.5, tm, TK)
        if i < n_a:
            mix, conv_new, ssm_new = gdn_fn(i, hn, tm)
            conv_states.append(conv_new)
            ssm_states.append(ssm_new)
            h, (hn,) = mm_res_norm(mix, gdn_w_out, (i,), h, ln_gain[i, 2][None], [BF16], 1.0, tm, TK)
        else:
            j = i - n_a
            q = mm(hn, diff_w_q, (j,), 0, d, tm, TN)
            att = attn_fn(j, i, q, kv)
            h, (hn,) = mm_res_norm(att, diff_w_out, (j,), h, ln_gain[i, 2][None], [BF16], 1.0, tm, TK)
        act = ffn_in(hn, ffn_w_in, i, 1, tm, TN)
        if i == depth - 1:
            _, (y,) = mm_res_norm(act, ffn_w_out, (i, 1), h, final_norm[None], [F32], 0.5, tm, TK)
        elif i == n_a - 1:
            gains = jnp.stack([kv_norm, ln_gain[i + 1, 0]])
            h, (hkv, hn) = mm_res_norm(act, ffn_w_out, (i, 1), h, gains, [BF16, BF16], 0.5, tm, TK)
            kv = kv_fn(hkv, tm)
            if n_drop:
                t = h.shape[0] // n_seq
                drop = lambda x: x.reshape(n_seq, t, d)[:, n_drop:].reshape(n_seq * (t - n_drop), d)
                h, hn = drop(h), drop(hn)
        else:
            h, (hn,) = mm_res_norm(act, ffn_w_out, (i, 1), h, ln_gain[i + 1, 0][None], [BF16], 0.5, tm, TK)
    return y, kv, jnp.stack(conv_states), jnp.stack(ssm_states)


def kernel(x_prompt, x_sample, cache_k, cache_v, state_conv, state_ssm, page_table, meta_tokens, ln_gain,
           ffn_w_in, ffn_w_out, gdn_w_in, gdn_conv, gdn_a_log, gdn_dt_bias, gdn_norm, gdn_w_out, kv_norm,
           w_kv, diff_w_q, diff_lambda, diff_subln, diff_w_out, rel_bias, final_norm):
    b, seq, d = x_prompt.shape
    db, dseq, _ = x_sample.shape
    n_a = gdn_w_in.shape[0]
    n_heads_v = gdn_a_log.shape[1]
    qk_dim = (n_heads_v // 2) * GDN_HEAD
    conv_dim = gdn_conv.shape[-1]
    main_cols = conv_dim + n_heads_v * GDN_HEAD
    n_meta = meta_tokens.shape[0]
    t_prompt = n_meta + seq
    kv_cols = w_kv.shape[1] // 2
    n_maps = rel_bias.shape[1]
    page = cache_k.shape[1]
    past_len = page_table.shape[1] * page
    assert n_meta == N_META and conv_dim == 2 * qk_dim + n_heads_v * GDN_HEAD
    lam_init = lambda layer: 0.8 - 0.6 * math.exp(-0.3 * layer)

    def gdn_proj(i, hn, tm):
        raw = mm(hn, gdn_w_in, (i,), 0, main_cols, tm, TN)
        bg = gdn_ba(hn, gdn_w_in[i][:, main_cols:], gdn_a_log[i], gdn_dt_bias[i], tm)
        return raw, bg

    def kv_proj(hkv, tm):
        return (mm(hkv, w_kv, (), 0, kv_cols, tm, TN), mm(hkv, w_kv, (), kv_cols, kv_cols, tm, TN))

    def gdn_prompt_fn(i, hn, tm):
        raw, bg = gdn_proj(i, hn, tm)
        raw3 = raw.reshape(b, t_prompt, main_cols)
        o, ssm = gdn_prompt(raw3, bg.reshape(b, t_prompt, -1), gdn_conv[i], gdn_norm[i], GDN_CHUNK,
                            GDN_HEADS_PER_STEP)
        return o.reshape(b * t_prompt, -1), raw3[:, t_prompt - 3:, :conv_dim], ssm

    prompt_bias = prompt_bias_tiles(rel_bias, ATTN_TILE, ATTN_TILE, n_meta)

    def attn_prompt_fn(j, layer, q, kv):
        k, v = kv
        o = attn_prompt(q.reshape(b, seq, d), k.reshape(b, t_prompt, kv_cols), v.reshape(b, t_prompt, kv_cols),
                        prompt_bias, rel_bias, diff_lambda[j], diff_subln[j], lam_init(layer), ATTN_TILE, ATTN_TILE)
        return o.reshape(b * seq, d)

    meta = jnp.broadcast_to(meta_tokens.astype(x_prompt.dtype)[None], (b, n_meta, d))
    h_prompt = jnp.concatenate([meta, x_prompt], axis=1).reshape(b * t_prompt, d)
    shared = (ln_gain, ffn_w_in, ffn_w_out, gdn_w_out, kv_norm, diff_w_q, diff_w_out, final_norm, n_a)
    y_p, (k_p, v_p), conv_p, ssm_p = _trunk(h_prompt, b, n_meta, gdn_prompt_fn, attn_prompt_fn, kv_proj,
                                            *shared, tm_target=768)

    def gdn_sample_fn(i, hn, tm):
        raw, bg = gdn_proj(i, hn, tm)
        raw3 = raw.reshape(db, dseq, main_cols)
        o, ssm = gdn_sample(raw3, bg.reshape(db, dseq, -1), state_conv[i], state_ssm[i], gdn_conv[i],
                            gdn_norm[i], GDN_SEQS_PER_STEP)
        return o.reshape(db * dseq, -1), raw3[:, dseq - 3:, :conv_dim], ssm

    sample_bias = sample_bias_tiles(rel_bias, dseq, page, past_len)
    cache_k2 = cache_k.reshape(cache_k.shape[0], page, kv_cols)
    cache_v2 = cache_v.reshape(cache_v.shape[0], page, kv_cols)

    def attn_sample_fn(j, layer, q, kv):
        k, v = kv
        r3 = lambda x: x.reshape(db, dseq, kv_cols)
        o = attn_sample(r3(q), r3(k), r3(v), cache_k2, cache_v2, page_table, sample_bias, diff_lambda[j],
                        diff_subln[j], lam_init(layer))
        return o.reshape(db * dseq, d)

    y_s, (k_s, v_s), conv_s, ssm_s = _trunk(x_sample.reshape(db * dseq, d), db, 0, gdn_sample_fn,
                                            attn_sample_fn, kv_proj, *shared, tm_target=512)

    hd = DIFF_HEAD
    return (y_p.reshape(b, seq, d), y_s.reshape(db, dseq, d),
            k_p.reshape(b, t_prompt, n_maps, hd), v_p.reshape(b, t_prompt, n_maps // 2, 2 * hd),
            conv_p, ssm_p,
            k_s.reshape(db, dseq, n_maps, hd), v_s.reshape(db, dseq, n_maps // 2, 2 * hd),
            conv_s, ssm_s)
```

```python
import functools
import math

import jax
import jax.numpy as jnp
from jax import lax
from jax.experimental import pallas as pl
from jax.experimental.pallas import tpu as pltpu

F32 = jnp.float32
BF16 = jnp.bfloat16

EPS = 1e-6
L2_EPS = 1e-6
N_META = 16
GDN_HEAD = 128
DIFF_HEAD = 128
NUM_BUCKETS = 32
MAX_DISTANCE = 128
NEG_BIG = -1e30
LANES = 128
VMEM_LIMIT = 56 * 1024 * 1024


def _params(*sem):
    return pltpu.CompilerParams(dimension_semantics=sem, vmem_limit_bytes=VMEM_LIMIT)


def _bdot(a, b):
    return jnp.dot(a.astype(BF16), b.astype(BF16), preferred_element_type=F32)


def _bdot_nt(a, b):
    return lax.dot_general(a.astype(BF16), b.astype(BF16), (((1,), (1,)), ((), ())),
                           preferred_element_type=F32)


def _bdot_tn(a, b):
    return lax.dot_general(a.astype(BF16), b.astype(BF16), (((0,), (0,)), ((), ())),
                           preferred_element_type=F32)


def _hdot(a, b):
    return jnp.dot(a, b, precision=lax.Precision.HIGHEST, preferred_element_type=F32)


def _split3(x):
    hi = x.astype(BF16)
    r1 = x - hi.astype(F32)
    mid = r1.astype(BF16)
    lo = (r1 - mid.astype(F32)).astype(BF16)
    return hi, mid, lo


def _exact_left_dot(mask01, x):
    m = mask01.astype(BF16)
    hi, mid, lo = _split3(x)
    d = functools.partial(jnp.dot, preferred_element_type=F32)
    return d(m, hi) + d(m, mid) + d(m, lo)


def _silu(x):
    return x * jax.nn.sigmoid(x)


def _rms_scale(x):
    return lax.rsqrt(jnp.mean(x * x, axis=-1, keepdims=True) + EPS)


def _rmsnorm_kernel(x_ref, g_ref, o_ref):
    x = x_ref[...].astype(F32)
    o_ref[...] = (x * _rms_scale(x) * g_ref[...]).astype(o_ref.dtype)


def rmsnorm(x, gain, tm):
    m, d = x.shape
    return pl.pallas_call(
        _rmsnorm_kernel,
        grid=(pl.cdiv(m, tm),),
        in_specs=[pl.BlockSpec((tm, d), lambda i: (i, 0)), pl.BlockSpec((1, d), lambda i: (0, 0))],
        out_specs=pl.BlockSpec((tm, d), lambda i: (i, 0)),
        out_shape=jax.ShapeDtypeStruct((m, d), BF16),
        compiler_params=_params("parallel"),
        name="rmsnorm",
    )(x, gain.reshape(1, d))


def _ffn_in_kernel(x_ref, wg_ref, wu_ref, o_ref):
    x = x_ref[...]
    g = jnp.dot(x, wg_ref[...].astype(BF16), preferred_element_type=F32)
    u = jnp.dot(x, wu_ref[...].astype(BF16), preferred_element_type=F32)
    o_ref[...] = (_silu(g) * u).astype(o_ref.dtype)


def ffn_in(x, w_in, layer, which, tm, tn):
    m, d = x.shape
    d_ff = w_in.shape[-1] // 2
    ncol = d_ff // tn
    assert d_ff % tn == 0
    wspec = lambda off: pl.BlockSpec((None, None, d, tn), lambda j, i: (layer, which, 0, j + off))
    return pl.pallas_call(
        _ffn_in_kernel,
        grid=(ncol, pl.cdiv(m, tm)),
        in_specs=[pl.BlockSpec((tm, d), lambda j, i: (i, 0)), wspec(0), wspec(ncol)],
        out_specs=pl.BlockSpec((tm, tn), lambda j, i: (i, j)),
        out_shape=jax.ShapeDtypeStruct((m, d_ff), BF16),
        compiler_params=_params("parallel", "parallel"),
        name="ffn_in",
    )(x, w_in, w_in)


def _mm_res_norm_kernel(a_ref, w_ref, h_ref, g_ref, oh_ref, *on_refs, scale, nk):
    k = pl.program_id(1)

    @pl.when(k == 0)
    def _():
        oh_ref[...] = h_ref[...]

    oh_ref[...] += scale * _bdot(a_ref[...], w_ref[...])

    @pl.when(k == nk - 1)
    def _():
        h = oh_ref[...]
        hs = h * _rms_scale(h)
        for n, on in enumerate(on_refs):
            on[...] = (hs * g_ref[n:n + 1, :]).astype(on.dtype)


def mm_res_norm(a, w, w_idx, h, gains, out_dtypes, scale, tm, tk):
    m, kdim = a.shape
    d = h.shape[-1]
    nk = kdim // tk
    assert kdim % tk == 0
    n_norm = gains.shape[0]
    lead = (None,) * len(w_idx)
    row = lambda i, k: (i, 0)
    outs = pl.pallas_call(
        functools.partial(_mm_res_norm_kernel, scale=scale, nk=nk),
        grid=(pl.cdiv(m, tm), nk),
        in_specs=[
            pl.BlockSpec((tm, tk), lambda i, k: (i, k)),
            pl.BlockSpec(lead + (tk, d), lambda i, k: tuple(w_idx) + (k, 0)),
            pl.BlockSpec((tm, d), row),
            pl.BlockSpec((n_norm, d), lambda i, k: (0, 0)),
        ],
        out_specs=[pl.BlockSpec((tm, d), row)] * (1 + n_norm),
        out_shape=[jax.ShapeDtypeStruct((m, d), F32)]
        + [jax.ShapeDtypeStruct((m, d), dt) for dt in out_dtypes],
        compiler_params=_params("parallel", "arbitrary"),
        name="mm_res_norm",
    )(a, w, h, gains)
    return outs[0], outs[1:]


def _mm_kernel(x_ref, w_ref, o_ref):
    o_ref[...] = _bdot(x_ref[...], w_ref[...]).astype(o_ref.dtype)


def mm(x, w, w_idx, col0, ncols, tm, tn):
    m, kdim = x.shape
    assert col0 % tn == 0 and ncols % tn == 0
    lead = (None,) * len(w_idx)
    c0 = col0 // tn
    return pl.pallas_call(
        _mm_kernel,
        grid=(ncols // tn, pl.cdiv(m, tm)),
        in_specs=[pl.BlockSpec((tm, kdim), lambda j, i: (i, 0)),
                  pl.BlockSpec(lead + (kdim, tn), lambda j, i: tuple(w_idx) + (0, c0 + j))],
        out_specs=pl.BlockSpec((tm, tn), lambda j, i: (i, j)),
        out_shape=jax.ShapeDtypeStruct((m, ncols), F32),
        compiler_params=_params("parallel", "parallel"),
        name="mm",
    )(x, w)


def _gdn_ba_kernel(x_ref, w_ref, alog_ref, dtb_ref, o_ref):
    r = _bdot(x_ref[...], w_ref[...])
    nh = r.shape[1] // 2
    lane = lax.broadcasted_iota(jnp.int32, r.shape, 1)
    xa = r + dtb_ref[...]
    softplus = jnp.maximum(xa, 0.0) + jnp.log1p(jnp.exp(-jnp.abs(xa)))
    g = -jnp.exp(alog_ref[...]) * softplus
    o_ref[...] = jnp.where(lane < nh, jax.nn.sigmoid(r), g)


def gdn_ba(x, w_ba, a_log, dt_bias, tm):
    m, d = x.shape
    nh = a_log.shape[0]
    pad = jnp.zeros((nh,), F32)
    alog = jnp.concatenate([pad, a_log.astype(F32)]).reshape(1, 2 * nh)
    dtb = jnp.concatenate([pad, dt_bias.astype(F32)]).reshape(1, 2 * nh)
    return pl.pallas_call(
        _gdn_ba_kernel,
        grid=(pl.cdiv(m, tm),),
        in_specs=[pl.BlockSpec((tm, d), lambda i: (i, 0)), pl.BlockSpec((d, 2 * nh), lambda i: (0, 0)),
                  pl.BlockSpec((1, 2 * nh), lambda i: (0, 0)), pl.BlockSpec((1, 2 * nh), lambda i: (0, 0))],
        out_specs=pl.BlockSpec((tm, 2 * nh), lambda i: (i, 0)),
        out_shape=jax.ShapeDtypeStruct((m, 2 * nh), F32),
        compiler_params=_params("parallel"),
        name="gdn_ba",
    )(x, w_ba, alog, dtb)


def _l2norm(x):
    return x * lax.rsqrt(jnp.sum(x * x, axis=-1, keepdims=True) + L2_EPS)


def _col(x, idx):
    lane = lax.broadcasted_iota(jnp.int32, x.shape, 1)
    return jnp.sum(jnp.where(lane == idx, x, 0.0), axis=-1, keepdims=True)


def _row_of(col):
    r = col.shape[0]
    return jnp.broadcast_to(col, (r, max(r, 128))).T[:r, :]


def _unit_lower_inverse_minus_eye(a_list, n_sq):
    p = [-a for a in a_list]
    t = list(p)
    for _ in range(n_sq):
        p = [_bdot(x, x) for x in p]
        tp = [_bdot(x, y) for x, y in zip(t, p)]
        t = [x + y + z for x, y, z in zip(t, p, tp)]
    return t


def _delta_wy(kk, qk, k, q, v, beta, gc, tril, strict, n_sq):
    n = len(v)
    decay = [jnp.exp(jnp.where(tril, gc[i] - _row_of(gc[i]), -jnp.inf)) for i in range(n)]
    a = [jnp.where(strict, beta[i] * kk[i] * decay[i], 0.0) for i in range(n)]
    t = _unit_lower_inverse_minus_eye(a, n_sq)
    egc = [jnp.exp(g) for g in gc]
    rhs = [jnp.concatenate([v[i] * beta[i], k[i] * (beta[i] * egc[i])], axis=1) for i in range(n)]
    trhs = [_bdot(t[i], rhs[i]) for i in range(n)]
    return ([rhs[i] + trhs[i] for i in range(n)], [q[i] * egc[i] for i in range(n)],
            [qk[i] * decay[i] for i in range(n)])


def _gdn_prompt_kernel(q_ref, k_ref, v_ref, z_ref, bg_ref, wq_ref, wk_ref, wv_ref, ng_ref,
                       o_ref, s_out_ref, qwin, kwin, vwin, s_ref, *, seq_len, rows, chunk, g_heads, n_sq):
    h = pl.program_id(1)
    c = pl.program_id(2)
    nc = pl.num_programs(2)
    dk = GDN_HEAD
    nb = rows // chunk
    lead = 8

    @pl.when(c == 0)
    def _():
        s_ref[...] = jnp.zeros_like(s_ref)
        for win in (qwin, kwin, vwin):
            win[pl.ds(0, lead), :] = jnp.zeros((lead, win.shape[1]), F32)

    row = lax.broadcasted_iota(jnp.int32, (rows, 1), 0)
    valid = (c * rows + row) < seq_len

    def conv(win, x_ref, w_ref):
        win[pl.ds(lead, rows), :] = x_ref[0]
        acc = win[pl.ds(lead - 3, rows), :] * w_ref[0:1, :]
        for i in range(1, 4):
            acc = acc + win[pl.ds(lead - 3 + i, rows), :] * w_ref[i:i + 1, :]
        win[pl.ds(lead - 3, 3), :] = win[pl.ds(lead + rows - 3, 3), :]
        return jnp.where(valid, _silu(acc), 0.0)

    qc = conv(qwin, q_ref, wq_ref)
    kc = conv(kwin, k_ref, wk_ref)
    vc = conv(vwin, v_ref, wv_ref)
    bg = jnp.where(valid, bg_ref[0], 0.0)
    nh = bg.shape[1] // 2

    ri = lax.broadcasted_iota(jnp.int32, (rows, rows), 0)
    ci = lax.broadcasted_iota(jnp.int32, (rows, rows), 1)
    same = (ri // chunk) == (ci // chunk)
    tril = same & (ri >= ci)
    strict = same & (ri > ci)
    gc_all = _exact_left_dot(tril, bg)

    nv = 2 * g_heads
    qs = [_l2norm(qc[:, gi * dk:(gi + 1) * dk]) * (dk ** -0.5) for gi in range(g_heads)]
    ks = [_l2norm(kc[:, gi * dk:(gi + 1) * dk]) for gi in range(g_heads)]
    kks = [_bdot_nt(x, x) for x in ks]
    qks = [_bdot_nt(x, y) for x, y in zip(qs, ks)]
    per_v = lambda xs: [xs[sv // 2] for sv in range(nv)]
    q, k = per_v(qs), per_v(ks)
    v = [vc[:, sv * dk:(sv + 1) * dk] for sv in range(nv)]
    beta = [_col(bg, 2 * h * g_heads + sv) for sv in range(nv)]
    gc = [_col(gc_all, nh + 2 * h * g_heads + sv) for sv in range(nv)]
    uw, qg, qkd = _delta_wy(per_v(kks), per_v(qks), k, q, v, beta, gc, tril, strict, n_sq)
    ow = [_bdot(x, y) for x, y in zip(qkd, uw)]
    q_eff = [qg[i] - ow[i][:, dk:] for i in range(nv)]
    s = [s_ref[sv] for sv in range(nv)]
    o_parts = [[] for _ in range(nv)]
    for j in range(nb):
        sl = slice(j * chunk, (j + 1) * chunk)
        g_end = [g[(j + 1) * chunk - 1:(j + 1) * chunk, :] for g in gc]
        kd = [k[i][sl] * jnp.exp(g_end[i] - gc[i][sl]) for i in range(nv)]
        bn = [_bdot_tn(kd[i], uw[i][sl]) for i in range(nv)]
        qs_j = [_bdot(q_eff[i][sl], s[i]) for i in range(nv)]
        ns_j = [_bdot(bn[i][:, dk:], s[i]) for i in range(nv)]
        for i in range(nv):
            o_parts[i].append(qs_j[i] + ow[i][sl, :dk])
            s[i] = s[i] * jnp.exp(g_end[i]) - ns_j[i] + bn[i][:, :dk]
    outs = []
    for sv in range(nv):
        s_ref[sv] = s[sv]
        o = o_parts[sv][0] if nb == 1 else jnp.concatenate(o_parts[sv], axis=0)
        z = z_ref[0, :, sv * dk:(sv + 1) * dk]
        outs.append(o * _rms_scale(o) * ng_ref[...] * _silu(z))
    o_ref[0] = jnp.concatenate(outs, axis=1).astype(o_ref.dtype)

    @pl.when(c == nc - 1)
    def _():
        s_out_ref[0] = s_ref[...]


def gdn_prompt(raw, bg, conv_w, norm_g, rows, chunk, g_heads):
    b, t, _ = raw.shape
    nh = bg.shape[-1] // 2
    dk = GDN_HEAD
    n_qk = nh // 2
    qk_dim = n_qk * dk
    gq, gv = g_heads * dk, 2 * g_heads * dk
    n_hblk = n_qk // g_heads
    nc = pl.cdiv(t, rows)
    n_sq = int(math.log2(chunk)) - 1
    assert 2 ** (n_sq + 1) == chunk and n_qk % g_heads == 0 and rows % chunk == 0
    blk = lambda width, off: pl.BlockSpec((1, rows, width), lambda bi, h, c: (bi, c, off + h))
    wblk = lambda width, off: pl.BlockSpec((4, width), lambda bi, h, c: (0, off + h))
    kernel = functools.partial(_gdn_prompt_kernel, seq_len=t, rows=rows, chunk=chunk, g_heads=g_heads,
                               n_sq=n_sq)
    return pl.pallas_call(
        kernel,
        grid=(b, n_hblk, nc),
        in_specs=[
            blk(gq, 0), blk(gq, n_hblk), blk(gv, n_hblk), blk(gv, 2 * n_hblk),
            pl.BlockSpec((1, rows, 2 * nh), lambda bi, h, c: (bi, c, 0)),
            wblk(gq, 0), wblk(gq, n_hblk), wblk(gv, n_hblk),
            pl.BlockSpec((1, dk), lambda bi, h, c: (0, 0)),
        ],
        out_specs=[
            pl.BlockSpec((1, rows, gv), lambda bi, h, c: (bi, c, h)),
            pl.BlockSpec((1, 2 * g_heads, dk, dk), lambda bi, h, c: (bi, h, 0, 0)),
        ],
        out_shape=[jax.ShapeDtypeStruct((b, t, 2 * qk_dim), BF16),
                   jax.ShapeDtypeStruct((b, nh, dk, dk), F32)],
        scratch_shapes=[pltpu.VMEM((8 + rows, gq), F32), pltpu.VMEM((8 + rows, gq), F32),
                        pltpu.VMEM((8 + rows, gv), F32), pltpu.VMEM((2 * g_heads, dk, dk), F32)],
        compiler_params=_params("parallel", "parallel", "arbitrary"),
        name="gdn_prompt",
    )(raw, raw, raw, raw, bg, conv_w, conv_w, conv_w, norm_g.reshape(1, dk))


def _gdn_sample_kernel(q_ref, k_ref, v_ref, z_ref, bg_ref, cq_ref, ck_ref, cv_ref, wq_ref, wk_ref, wv_ref,
                       ng_ref, s_in_ref, o_ref, s_out_ref, qwin, kwin, vwin, *, n_sq):
    h = pl.program_id(0)
    tb, t, dk = q_ref.shape
    r = tb * t
    lead = 8

    def conv(win, x_ref, c_ref, w_ref):
        win[:, pl.ds(lead - 3, 3), :] = c_ref[...]
        win[:, pl.ds(lead, t), :] = x_ref[...]
        acc = win[:, pl.ds(lead - 3, t), :] * w_ref[0:1, :]
        for i in range(1, 4):
            acc = acc + win[:, pl.ds(lead - 3 + i, t), :] * w_ref[i:i + 1, :]
        return _silu(acc).reshape(r, x_ref.shape[-1])

    qc = conv(qwin, q_ref, cq_ref, wq_ref)
    kc = conv(kwin, k_ref, ck_ref, wk_ref)
    vc = conv(vwin, v_ref, cv_ref, wv_ref)
    bg = bg_ref[...].reshape(r, bg_ref.shape[-1])
    nh = bg.shape[1] // 2

    ri = lax.broadcasted_iota(jnp.int32, (r, r), 0)
    ci = lax.broadcasted_iota(jnp.int32, (r, r), 1)
    same = (ri // t) == (ci // t)
    tril = same & (ri >= ci)
    strict = same & (ri > ci)
    gc_all = _exact_left_dot(tril, bg)
    g_sum = jnp.sum(bg_ref[...], axis=1, keepdims=True)

    q = _l2norm(qc) * (dk ** -0.5)
    k = _l2norm(kc)
    kk = _bdot_nt(k, k)
    qk = _bdot_nt(q, k)
    bdims = (((2,), (1,)), ((0,), (0,)))
    two = range(2)
    v = [vc[:, e * dk:(e + 1) * dk] for e in two]
    beta = [_col(bg, 2 * h + e) for e in two]
    gc = [_col(gc_all, nh + 2 * h + e) for e in two]
    uw, qg, qkd = _delta_wy([kk, kk], [qk, qk], [k, k], [q, q], v, beta, gc, tril, strict, n_sq)
    s = [s_in_ref[:, e] for e in two]
    wq3 = [jnp.concatenate([uw[e][:, dk:].reshape(tb, t, dk), qg[e].reshape(tb, t, dk)], axis=1).astype(BF16)
           for e in two]
    ws_qs = [lax.dot_general(wq3[e], s[e].astype(BF16), bdims, preferred_element_type=F32) for e in two]
    v_new = [uw[e][:, :dk] - ws_qs[e][:, :t].reshape(r, dk) for e in two]
    qkv = [_bdot(qkd[e], v_new[e]) for e in two]
    lane3 = lax.broadcasted_iota(jnp.int32, g_sum.shape, 2)
    g_last = [jnp.sum(jnp.where(lane3 == nh + 2 * h + e, g_sum, 0.0), axis=-1, keepdims=True) for e in two]
    kd = [(k.reshape(tb, t, dk) * jnp.exp(g_last[e] - gc[e].reshape(tb, t, 1))).astype(BF16) for e in two]
    upd = [lax.dot_general(kd[e], v_new[e].reshape(tb, t, dk).astype(BF16), (((1,), (1,)), ((0,), (0,))),
                           preferred_element_type=F32) for e in two]
    for e in two:
        s_out_ref[:, e] = s[e] * jnp.exp(g_last[e]) + upd[e]
        o = ws_qs[e][:, t:].reshape(r, dk) + qkv[e]
        z = z_ref[:, :, e * dk:(e + 1) * dk].reshape(r, dk)
        o_ref[:, :, e * dk:(e + 1) * dk] = (o * _rms_scale(o) * ng_ref[...] * _silu(z)).reshape(tb, t, dk)


def gdn_sample(raw, bg, conv_state, ssm_state, conv_w, norm_g, tb):
    b, t, _ = raw.shape
    nh = bg.shape[-1] // 2
    dk = GDN_HEAD
    n_qk = nh // 2
    n_sq = int(math.log2(t)) - 1
    assert 2 ** (n_sq + 1) == t and b % tb == 0 and t == 8
    blk = lambda width, off: pl.BlockSpec((tb, t, width), lambda h, i: (i, 0, off + h))
    cblk = lambda width, off: pl.BlockSpec((tb, 3, width), lambda h, i: (i, 0, off + h))
    wblk = lambda width, off: pl.BlockSpec((4, width), lambda h, i: (0, off + h))
    sblk = pl.BlockSpec((tb, 2, dk, dk), lambda h, i: (i, h, 0, 0))
    return pl.pallas_call(
        functools.partial(_gdn_sample_kernel, n_sq=n_sq),
        grid=(n_qk, b // tb),
        in_specs=[
            blk(dk, 0), blk(dk, n_qk), blk(2 * dk, n_qk), blk(2 * dk, 2 * n_qk),
            pl.BlockSpec((tb, t, 2 * nh), lambda h, i: (i, 0, 0)),
            cblk(dk, 0), cblk(dk, n_qk), cblk(2 * dk, n_qk),
            wblk(dk, 0), wblk(dk, n_qk), wblk(2 * dk, n_qk),
            pl.BlockSpec((1, dk), lambda h, i: (0, 0)),
            sblk,
        ],
        out_specs=[pl.BlockSpec((tb, t, 2 * dk), lambda h, i: (i, 0, h)), sblk],
        out_shape=[jax.ShapeDtypeStruct((b, t, nh * dk), F32),
                   jax.ShapeDtypeStruct(ssm_state.shape, F32)],
        scratch_shapes=[pltpu.VMEM((tb, 8 + t, dk), F32), pltpu.VMEM((tb, 8 + t, dk), F32),
                        pltpu.VMEM((tb, 8 + t, 2 * dk), F32)],
        compiler_params=_params("parallel", "parallel"),
        name="gdn_sample",
    )(raw, raw, raw, raw, bg, conv_state, conv_state, conv_state, conv_w, conv_w, conv_w,
      norm_g.reshape(1, dk), ssm_state)


def _t5_bucket(dist):
    max_exact = NUM_BUCKETS // 2
    n = jnp.maximum(dist, 0)
    log_ratio = jnp.log(jnp.maximum(n, 1).astype(F32) * (1.0 / max_exact)) * (1.0 / math.log(MAX_DISTANCE / max_exact))
    large = jnp.minimum(max_exact + (log_ratio * (NUM_BUCKETS - max_exact)).astype(jnp.int32), NUM_BUCKETS - 1)
    return jnp.where(n < max_exact, n, large)


def _prompt_bias_kernel(rb_ref, o_ref, *, tq, tk, n_meta):
    ty = pl.program_id(0)
    m = pl.program_id(1)
    ii = lax.broadcasted_iota(jnp.int32, (tq, tk), 0)
    jj = lax.broadcasted_iota(jnp.int32, (tq, tk), 1)
    dist = n_meta + (ty - 1) * tk + ii - jj
    bucket = _t5_bucket(dist)
    tile = jnp.full((tq, tk), NEG_BIG, F32)
    for bkt in range(NUM_BUCKETS):
        tile = jnp.where((bucket == bkt) & (dist >= 0), rb_ref[bkt, m], tile)
    o_ref[0, 0] = tile


def prompt_bias_tiles(rel_bias, tq, tk, n_meta):
    n_maps = rel_bias.shape[1]
    return pl.pallas_call(
        functools.partial(_prompt_bias_kernel, tq=tq, tk=tk, n_meta=n_meta),
        grid=(3, n_maps),
        in_specs=[pl.BlockSpec(memory_space=pltpu.SMEM)],
        out_specs=pl.BlockSpec((1, 1, tq, tk), lambda ty, m: (ty, m, 0, 0)),
        out_shape=jax.ShapeDtypeStruct((3, n_maps, tq, tk), F32),
        compiler_params=_params("parallel", "parallel"),
        name="prompt_bias",
    )(rel_bias.astype(F32))


def _lambda_of(lv_ref, lam_init):
    lv = lv_ref[...].astype(F32)
    a = jnp.sum(lv[0:1] * lv[1:2], axis=-1, keepdims=True)
    b = jnp.sum(lv[2:3] * lv[3:4], axis=-1, keepdims=True)
    return jnp.exp(a) - jnp.exp(b) + lam_init


def _attn_prompt_kernel(qi_ref, kj_ref, q_ref, k_ref, v_ref, bias_ref, rb_ref, lv_ref, sub_ref, o_ref,
                        m_ref, l_ref, acc_ref, *, n_keys, nk, tk, hp, lam_init, scale):
    h = pl.program_id(1)
    step = pl.program_id(2)
    qi = qi_ref[step]
    kj = kj_ref[step]
    dh = DIFF_HEAD
    hw = 2 * dh

    @pl.when(kj == 0)
    def _():
        m_ref[...] = jnp.full_like(m_ref, NEG_BIG)
        l_ref[...] = jnp.zeros_like(l_ref)
        acc_ref[...] = jnp.zeros_like(acc_ref)

    krow = lax.broadcasted_iota(jnp.int32, (tk, 1), 0)
    kvalid = (kj * tk + krow) < n_keys
    k = jnp.where(kvalid, k_ref[0], 0.0).astype(BF16)
    v = jnp.where(kvalid, v_ref[0], 0.0).astype(BF16)
    near = (qi - kj) <= 1
    lanes = m_ref.shape[-1]
    wide = lambda x, n: jnp.concatenate([x] * (n // lanes), axis=1)
    ones = jnp.ones((tk, lanes), BF16)
    maps = range(2 * hp)
    s = [_bdot_nt((q_ref[0, :, mi * dh:(mi + 1) * dh] * scale).astype(BF16), k[:, mi * dh:(mi + 1) * dh])
         for mi in maps]
    ps, corrs = [], []
    for mi in maps:
        far_bias = rb_ref[NUM_BUCKETS - 1, 2 * hp * h + mi]
        sb = s[mi] + jnp.where(near, bias_ref[0, mi], far_bias)
        m_old = m_ref[mi]
        m_new = jnp.maximum(m_old, jnp.max(sb, axis=-1, keepdims=True))
        m_ref[mi] = m_new
        corrs.append(jnp.exp(m_old - m_new))
        ps.append(jnp.exp(sb - wide(m_new, tk)).astype(BF16))
    sums = [jnp.dot(ps[mi], ones, preferred_element_type=F32) for mi in maps]
    pvs = [jnp.dot(ps[mi], v[:, (mi // 2) * hw:(mi // 2 + 1) * hw], preferred_element_type=F32) for mi in maps]
    for mi in maps:
        l_ref[mi] = l_ref[mi] * corrs[mi] + sums[mi]
        acc_ref[mi] = acc_ref[mi] * wide(corrs[mi], hw) + pvs[mi]

    @pl.when(kj == jnp.minimum(qi + 1, nk - 1))
    def _():
        lam = _lambda_of(lv_ref, lam_init)
        for hh in range(hp):
            o = (acc_ref[2 * hh] / wide(l_ref[2 * hh], hw)
                 - lam * (acc_ref[2 * hh + 1] / wide(l_ref[2 * hh + 1], hw)))
            o_ref[0, :, hh * hw:(hh + 1) * hw] = (
                o * _rms_scale(o) * sub_ref[...] * (1.0 - lam_init)).astype(o_ref.dtype)


def attn_prompt(q, k, v, bias_tiles, rel_bias, lam_vecs, subln, lam_init, tq, tk, hp):
    b, s, d = q.shape
    n_keys = k.shape[1]
    n_heads = d // (2 * DIFF_HEAD)
    hw = 2 * DIFF_HEAD
    bw = hp * hw
    nq, nk = s // tq, pl.cdiv(n_keys, tk)
    assert s % tq == 0 and tq == tk and N_META <= tk and n_heads % hp == 0
    assert N_META + 2 * tk - (tk - 1) >= MAX_DISTANCE
    pairs = [(qi, kj) for qi in range(nq) for kj in range(min(qi + 1, nk - 1) + 1)]
    qi_tab = jnp.asarray([p[0] for p in pairs], jnp.int32)
    kj_tab = jnp.asarray([p[1] for p in pairs], jnp.int32)

    qmap = lambda bi, h, st, qt, kt: (bi, qt[st], h)
    kvmap = lambda bi, h, st, qt, kt: (bi, kt[st], h)
    bias_map = lambda bi, h, st, qt, kt: (jnp.clip(qt[st] - kt[st] + 1, 0, 2), h, 0, 0)
    const2 = lambda bi, h, st, qt, kt: (0, 0)
    kernel = functools.partial(_attn_prompt_kernel, n_keys=n_keys, nk=nk, tk=tk, hp=hp, lam_init=lam_init,
                               scale=DIFF_HEAD ** -0.5)
    grid_spec = pltpu.PrefetchScalarGridSpec(
        num_scalar_prefetch=2,
        grid=(b, n_heads // hp, len(pairs)),
        in_specs=[
            pl.BlockSpec((1, tq, bw), qmap),
            pl.BlockSpec((1, tk, bw), kvmap),
            pl.BlockSpec((1, tk, bw), kvmap),
            pl.BlockSpec((1, 2 * hp, tq, tk), bias_map),
            pl.BlockSpec(memory_space=pltpu.SMEM),
            pl.BlockSpec(lam_vecs.shape, const2),
            pl.BlockSpec((1, hw), const2),
        ],
        out_specs=pl.BlockSpec((1, tq, bw), qmap),
        scratch_shapes=[pltpu.VMEM((2 * hp, tq, LANES), F32), pltpu.VMEM((2 * hp, tq, LANES), F32),
                        pltpu.VMEM((2 * hp, tq, hw), F32)],
    )
    return pl.pallas_call(
        kernel,
        grid_spec=grid_spec,
        out_shape=jax.ShapeDtypeStruct((b, s, d), BF16),
        compiler_params=_params("parallel", "parallel", "arbitrary"),
        name="attn_prompt",
    )(qi_tab, kj_tab, q, k, v, bias_tiles, rel_bias.astype(F32), lam_vecs, subln.reshape(1, hw))


def _sample_bias_kernel(rb_ref, o_ref, *, t, page, past_len):
    r = o_ref.shape[1]
    ri = lax.broadcasted_iota(jnp.int32, (r, page), 0)
    jj = lax.broadcasted_iota(jnp.int32, (r, page), 1)
    tq = ri % t
    far = rb_ref[:, NUM_BUCKETS - 1:NUM_BUCKETS]
    o_ref[0] = jnp.broadcast_to(far, (r, page))
    for ty, dist, ok in ((1, page + tq - jj, None), (2, tq - jj, (jj <= tq) & (jj < t))):
        bucket = _t5_bucket(dist)
        tile = jnp.full((r, page), NEG_BIG, F32)
        for bkt in range(NUM_BUCKETS):
            hit = (bucket == bkt) if ok is None else ((bucket == bkt) & ok)
            tile = jnp.where(hit, rb_ref[:, bkt:bkt + 1], tile)
        o_ref[ty] = tile
    del past_len


def sample_bias_tiles(rel_bias, t, page, past_len):
    n_maps = rel_bias.shape[1]
    assert past_len % page == 0 and past_len - page >= MAX_DISTANCE
    rb_rows = jnp.repeat(rel_bias.astype(F32).T, t, axis=0)
    return pl.pallas_call(
        functools.partial(_sample_bias_kernel, t=t, page=page, past_len=past_len),
        out_shape=jax.ShapeDtypeStruct((3, n_maps * t, page), F32),
        compiler_params=pltpu.CompilerParams(vmem_limit_bytes=VMEM_LIMIT),
        name="sample_bias",
    )(rb_rows)


def _attn_sample_kernel(pt_ref, q_ref, kc_ref, vlo_ref, vhi_ref, kn_ref, vn_ref, bias_ref, nbias_ref, lv_ref, sub_ref,
                        o_ref, m_ref, l_ref, acc_ref, *, lam_init, scale, page):
    del pt_ref
    p = pl.program_id(1)
    n_pages = pl.num_programs(1)
    t, d = q_ref.shape[1], q_ref.shape[2]
    dh = DIFF_HEAD
    hw = 2 * dh
    n_maps = d // dh
    n_heads = n_maps // 2

    @pl.when(p == 0)
    def _():
        m_ref[...] = jnp.full_like(m_ref, NEG_BIG)
        l_ref[...] = jnp.zeros_like(l_ref)
        acc_ref[...] = jnp.zeros_like(acc_ref)

    def update(k_of, v_of, bias):
        s = jnp.concatenate([_bdot_nt(q_ref[0, :, m * dh:(m + 1) * dh] * scale, k_of(m))
                             for m in range(n_maps)], axis=0) + bias
        m_old = m_ref[...]
        m_new = jnp.maximum(m_old, jnp.max(s, axis=-1, keepdims=True))
        corr = jnp.exp(m_old - m_new)
        pr = jnp.exp(s - m_new)
        l_ref[...] = l_ref[...] * corr + jnp.sum(pr, axis=-1, keepdims=True)
        pv = jnp.concatenate([_bdot(pr[2 * h * t:(2 * h + 2) * t], v_of(h)) for h in range(n_heads)], axis=0)
        acc_ref[...] = acc_ref[...] * corr + pv
        m_ref[...] = m_new

    update(lambda m: kc_ref[pl.ds(m, page, stride=n_maps), :],
           lambda h: jnp.concatenate([vlo_ref[pl.ds(h, page, stride=n_heads), :],
                                      vhi_ref[pl.ds(h, page, stride=n_heads), :]], axis=1), bias_ref[0])

    @pl.when(p == n_pages - 1)
    def _():
        kn, vn = kn_ref[0], vn_ref[0]
        zk = jnp.zeros((page - t, dh), F32)
        zv = jnp.zeros((page - t, hw), F32)
        update(lambda m: jnp.concatenate([kn[:, m * dh:(m + 1) * dh], zk], axis=0),
               lambda h: jnp.concatenate([vn[:, h * hw:(h + 1) * hw], zv], axis=0), nbias_ref[0])
        lam = _lambda_of(lv_ref, lam_init)
        for h in range(n_heads):
            r0 = 2 * h * t
            o0 = acc_ref[pl.ds(r0, t), :] / l_ref[pl.ds(r0, t), :]
            o1 = acc_ref[pl.ds(r0 + t, t), :] / l_ref[pl.ds(r0 + t, t), :]
            o = o0 - lam * o1
            o_ref[0, :, h * hw:(h + 1) * hw] = o * _rms_scale(o) * sub_ref[...] * (1.0 - lam_init)


def attn_sample(q, k_new, v_new, cache_k, cache_v, page_table, bias_tiles, lam_vecs, subln, lam_init):
    b, t, d = q.shape
    n_pages = page_table.shape[1]
    dh = DIFF_HEAD
    hw = 2 * dh
    n_maps = d // dh
    page = cache_k.shape[1] // n_maps
    r = n_maps * t
    assert r == bias_tiles.shape[1] and page == bias_tiles.shape[2] and cache_v.shape[1] == page * n_maps // 2

    def cache_map(bi, p, pt):
        return (pt[bi, p], 0, 0)

    def bias_map(bi, p, pt):
        return (jnp.where(p == n_pages - 1, 1, 0), 0, 0)

    row = lambda bi, p, pt: (bi, 0, 0)
    grid_spec = pltpu.PrefetchScalarGridSpec(
        num_scalar_prefetch=1,
        grid=(b, n_pages),
        in_specs=[
            pl.BlockSpec((1, t, d), row),
            pl.BlockSpec((None, page * n_maps, dh), cache_map),
            pl.BlockSpec((None, page * n_maps // 2, dh), cache_map),
            pl.BlockSpec((None, page * n_maps // 2, dh), lambda bi, p, pt: (pt[bi, p], 0, 1)),
            pl.BlockSpec((1, t, d), row),
            pl.BlockSpec((1, t, d), row),
            pl.BlockSpec((1, r, page), bias_map),
            pl.BlockSpec((1, r, page), lambda bi, p, pt: (2, 0, 0)),
            pl.BlockSpec(lam_vecs.shape, lambda bi, p, pt: (0, 0)),
            pl.BlockSpec((1, hw), lambda bi, p, pt: (0, 0)),
        ],
        out_specs=pl.BlockSpec((1, t, d), row),
        scratch_shapes=[pltpu.VMEM((r, 1), F32), pltpu.VMEM((r, 1), F32), pltpu.VMEM((r, hw), F32)],
    )
    return pl.pallas_call(
        functools.partial(_attn_sample_kernel, lam_init=lam_init, scale=dh ** -0.5, page=page),
        grid_spec=grid_spec,
        out_shape=jax.ShapeDtypeStruct((b, t, d), F32),
        compiler_params=_params("parallel", "arbitrary"),
        name="attn_sample",
    )(page_table, q, cache_k, cache_v, cache_v, k_new, v_new, bias_tiles, bias_tiles, lam_vecs,
      subln.reshape(1, hw))


TN = 512
TN_MM = 1024
TK = 512
ATTN_TILE = 256
ATTN_HEADS_PER_STEP = 2
GDN_CHUNK = 64
GDN_ROWS = 128
GDN_HEADS_PER_STEP = 4
GDN_SEQS_PER_STEP = 16


def _row_tile(m, target):
    best = 16
    for t in range(16, target + 1, 16):
        if m % t == 0:
            best = t
    assert m % best == 0
    return best


def _trunk(h, n_seq, n_drop, gdn_fn, attn_fn, kv_fn, ln_gain, ffn_w_in, ffn_w_out, gdn_w_out, kv_norm,
           diff_w_q, diff_w_out, final_norm, n_a, tm_target):
    depth = ln_gain.shape[0]
    d = h.shape[-1]
    tm = _row_tile(h.shape[0], tm_target)
    hn = rmsnorm(h, ln_gain[0, 0], tm)
    conv_states, ssm_states, kv = [], [], None
    y = None
    for i in range(depth):
        tm = _row_tile(h.shape[0], tm_target)
        act = ffn_in(hn, ffn_w_in, i, 0, tm, TN)
        h, (hn,) = mm_res_norm(act, ffn_w_out, (i, 0), h, ln_gain[i, 1][None], [BF16], 0.5, tm, TK)
        if i < n_a:
            mix, conv_new, ssm_new = gdn_fn(i, hn, tm)
            conv_states.append(conv_new)
            ssm_states.append(ssm_new)
            h, (hn,) = mm_res_norm(mix, gdn_w_out, (i,), h, ln_gain[i, 2][None], [BF16], 1.0, tm, TK)
        else:
            j = i - n_a
            q = mm(hn, diff_w_q, (j,), 0, d, tm, TN_MM)
            att = attn_fn(j, i, q, kv)
            h, (hn,) = mm_res_norm(att, diff_w_out, (j,), h, ln_gain[i, 2][None], [BF16], 1.0, tm, TK)
        act = ffn_in(hn, ffn_w_in, i, 1, tm, TN)
        if i == depth - 1:
            _, (y,) = mm_res_norm(act, ffn_w_out, (i, 1), h, final_norm[None], [F32], 0.5, tm, TK)
        elif i == n_a - 1:
            gains = jnp.stack([kv_norm, ln_gain[i + 1, 0]])
            h, (hkv, hn) = mm_res_norm(act, ffn_w_out, (i, 1), h, gains, [BF16, BF16], 0.5, tm, TK)
            kv = kv_fn(hkv, tm)
            if n_drop:
                t = h.shape[0] // n_seq
                drop = lambda x: x.reshape(n_seq, t, d)[:, n_drop:].reshape(n_seq * (t - n_drop), d)
                h, hn = drop(h), drop(hn)
        else:
            h, (hn,) = mm_res_norm(act, ffn_w_out, (i, 1), h, ln_gain[i + 1, 0][None], [BF16], 0.5, tm, TK)
    return y, kv, jnp.stack(conv_states), jnp.stack(ssm_states)


def kernel(x_prompt, x_sample, cache_k, cache_v, state_conv, state_ssm, page_table, meta_tokens, ln_gain,
           ffn_w_in, ffn_w_out, gdn_w_in, gdn_conv, gdn_a_log, gdn_dt_bias, gdn_norm, gdn_w_out, kv_norm,
           w_kv, diff_w_q, diff_lambda, diff_subln, diff_w_out, rel_bias, final_norm):
    b, seq, d = x_prompt.shape
    db, dseq, _ = x_sample.shape
    n_a = gdn_w_in.shape[0]
    n_heads_v = gdn_a_log.shape[1]
    qk_dim = (n_heads_v // 2) * GDN_HEAD
    conv_dim = gdn_conv.shape[-1]
    main_cols = conv_dim + n_heads_v * GDN_HEAD
    n_meta = meta_tokens.shape[0]
    t_prompt = n_meta + seq
    kv_cols = w_kv.shape[1] // 2
    n_maps = rel_bias.shape[1]
    page = cache_k.shape[1]
    past_len = page_table.shape[1] * page
    assert n_meta == N_META and conv_dim == 2 * qk_dim + n_heads_v * GDN_HEAD
    lam_init = lambda layer: 0.8 - 0.6 * math.exp(-0.3 * layer)

    def gdn_proj(i, hn, tm):
        raw = mm(hn, gdn_w_in, (i,), 0, main_cols, tm, TN_MM)
        bg = gdn_ba(hn, gdn_w_in[i][:, main_cols:], gdn_a_log[i], gdn_dt_bias[i], tm)
        return raw, bg

    def kv_proj(hkv, tm):
        return (mm(hkv, w_kv, (), 0, kv_cols, tm, TN_MM), mm(hkv, w_kv, (), kv_cols, kv_cols, tm, TN_MM))

    def gdn_prompt_fn(i, hn, tm):
        raw, bg = gdn_proj(i, hn, tm)
        raw3 = raw.reshape(b, t_prompt, main_cols)
        o, ssm = gdn_prompt(raw3, bg.reshape(b, t_prompt, -1), gdn_conv[i], gdn_norm[i], GDN_ROWS, GDN_CHUNK,
                            GDN_HEADS_PER_STEP)
        return o.reshape(b * t_prompt, -1), raw3[:, t_prompt - 3:, :conv_dim], ssm

    prompt_bias = prompt_bias_tiles(rel_bias, ATTN_TILE, ATTN_TILE, n_meta)

    def attn_prompt_fn(j, layer, q, kv):
        k, v = kv
        o = attn_prompt(q.reshape(b, seq, d), k.reshape(b, t_prompt, kv_cols), v.reshape(b, t_prompt, kv_cols),
                        prompt_bias, rel_bias, diff_lambda[j], diff_subln[j], lam_init(layer), ATTN_TILE, ATTN_TILE,
                        ATTN_HEADS_PER_STEP)
        return o.reshape(b * seq, d)

    meta = jnp.broadcast_to(meta_tokens.astype(x_prompt.dtype)[None], (b, n_meta, d))
    h_prompt = jnp.concatenate([meta, x_prompt], axis=1).reshape(b * t_prompt, d)
    shared = (ln_gain, ffn_w_in, ffn_w_out.astype(BF16), gdn_w_out.astype(BF16), kv_norm, diff_w_q,
              diff_w_out.astype(BF16), final_norm, n_a)
    y_p, (k_p, v_p), conv_p, ssm_p = _trunk(h_prompt, b, n_meta, gdn_prompt_fn, attn_prompt_fn, kv_proj,
                                            *shared, tm_target=768)

    def gdn_sample_fn(i, hn, tm):
        raw, bg = gdn_proj(i, hn, tm)
        raw3 = raw.reshape(db, dseq, main_cols)
        o, ssm = gdn_sample(raw3, bg.reshape(db, dseq, -1), state_conv[i], state_ssm[i], gdn_conv[i],
                            gdn_norm[i], GDN_SEQS_PER_STEP)
        return o.reshape(db * dseq, -1), raw3[:, dseq - 3:, :conv_dim], ssm

    sample_bias = sample_bias_tiles(rel_bias, dseq, page, past_len)
    cache_k2 = cache_k.reshape(cache_k.shape[0], page * cache_k.shape[2], cache_k.shape[3])
    cache_v2 = cache_v.reshape(cache_v.shape[0], page * cache_v.shape[2], cache_v.shape[3])

    def attn_sample_fn(j, layer, q, kv):
        k, v = kv
        r3 = lambda x: x.reshape(db, dseq, kv_cols)
        o = attn_sample(r3(q), r3(k), r3(v), cache_k2, cache_v2, page_table, sample_bias, diff_lambda[j],
                        diff_subln[j], lam_init(layer))
        return o.reshape(db * dseq, d)

    y_s, (k_s, v_s), conv_s, ssm_s = _trunk(x_sample.reshape(db * dseq, d), db, 0, gdn_sample_fn,
                                            attn_sample_fn, kv_proj, *shared, tm_target=1024)

    hd = DIFF_HEAD
    return (y_p.reshape(b, seq, d), y_s.reshape(db, dseq, d),
            k_p.reshape(b, t_prompt, n_maps, hd), v_p.reshape(b, t_prompt, n_maps // 2, 2 * hd),
            conv_p, ssm_p,
            k_s.reshape(db, dseq, n_maps, hd), v_s.reshape(db, dseq, n_maps // 2, 2 * hd),
            conv_s, ssm_s)
```

```python
import functools
import math

import jax
import jax.numpy as jnp
from jax import lax
from jax.experimental import pallas as pl
from jax.experimental.pallas import tpu as pltpu

F32 = jnp.float32
BF16 = jnp.bfloat16

EPS = 1e-6
L2_EPS = 1e-6
N_META = 16
GDN_HEAD = 128
GDN_CONV_TAPS = 4
DIFF_HEAD = 128
NUM_BUCKETS = 32
MAX_DISTANCE = 128
NEG_BIG = -1e30
LANES = 128
VMEM_LIMIT = 56 * 1024 * 1024
VMEM_BUDGET = 44 * 1024 * 1024
RES_TM_MAX = 768


def _params(*sem):
    return pltpu.CompilerParams(dimension_semantics=sem, vmem_limit_bytes=VMEM_LIMIT)


def _bdot(a, b):
    return jnp.dot(a.astype(BF16), b.astype(BF16), preferred_element_type=F32)


def _bdot_nt(a, b):
    return lax.dot_general(a.astype(BF16), b.astype(BF16), (((1,), (1,)), ((), ())),
                           preferred_element_type=F32)


def _bdot_tn(a, b):
    return lax.dot_general(a.astype(BF16), b.astype(BF16), (((0,), (0,)), ((), ())),
                           preferred_element_type=F32)


def _hdot(a, b):
    return jnp.dot(a, b, precision=lax.Precision.HIGHEST, preferred_element_type=F32)


def _split3(x):
    hi = x.astype(BF16)
    r1 = x - hi.astype(F32)
    mid = r1.astype(BF16)
    lo = (r1 - mid.astype(F32)).astype(BF16)
    return hi, mid, lo


def _exact_left_dot(mask01, x):
    m = mask01.astype(BF16)
    hi, mid, lo = _split3(x)
    d = functools.partial(jnp.dot, preferred_element_type=F32)
    return d(m, hi) + d(m, mid) + d(m, lo)


def _silu(x):
    return x * jax.nn.sigmoid(x)


def _rms_scale(x):
    return lax.rsqrt(jnp.mean(x * x, axis=-1, keepdims=True) + EPS)


def _rmsnorm_kernel(x_ref, g_ref, o_ref):
    x = x_ref[...].astype(F32)
    o_ref[...] = (x * _rms_scale(x) * g_ref[...]).astype(o_ref.dtype)


def rmsnorm(x, gain, tm):
    m, d = x.shape
    return pl.pallas_call(
        _rmsnorm_kernel,
        grid=(pl.cdiv(m, tm),),
        in_specs=[pl.BlockSpec((tm, d), lambda i: (i, 0)), pl.BlockSpec((1, d), lambda i: (0, 0))],
        out_specs=pl.BlockSpec((tm, d), lambda i: (i, 0)),
        out_shape=jax.ShapeDtypeStruct((m, d), BF16),
        compiler_params=_params("parallel"),
        name="rmsnorm",
    )(x, gain.reshape(1, d))


def _ffn_in_kernel(x_ref, wg_ref, wu_ref, o_ref):
    x = x_ref[...]
    g = jnp.dot(x, wg_ref[...].astype(BF16), preferred_element_type=F32)
    u = jnp.dot(x, wu_ref[...].astype(BF16), preferred_element_type=F32)
    o_ref[...] = (_silu(g) * u).astype(o_ref.dtype)


def ffn_in(x, w_in, layer, which, tm, tn):
    m, d = x.shape
    d_ff = w_in.shape[-1] // 2
    ncol = d_ff // tn
    assert d_ff % tn == 0
    wspec = lambda off: pl.BlockSpec((None, None, d, tn), lambda j, i: (layer, which, 0, j + off))
    return pl.pallas_call(
        _ffn_in_kernel,
        grid=(ncol, pl.cdiv(m, tm)),
        in_specs=[pl.BlockSpec((tm, d), lambda j, i: (i, 0)), wspec(0), wspec(ncol)],
        out_specs=pl.BlockSpec((tm, tn), lambda j, i: (i, j)),
        out_shape=jax.ShapeDtypeStruct((m, d_ff), BF16),
        compiler_params=_params("parallel", "parallel"),
        name="ffn_in",
    )(x, w_in, w_in)


def _mm_res_norm_kernel(a_ref, w_ref, h_ref, g_ref, oh_ref, *on_refs, scale, nk):
    k = pl.program_id(1)

    @pl.when(k == 0)
    def _():
        oh_ref[...] = h_ref[...]

    oh_ref[...] += scale * _bdot(a_ref[...], w_ref[...])

    @pl.when(k == nk - 1)
    def _():
        h = oh_ref[...]
        hs = h * _rms_scale(h)
        for n, on in enumerate(on_refs):
            on[...] = (hs * g_ref[n:n + 1, :]).astype(on.dtype)


def _res_norm_tiles(m, kdim, d, a_dtype, w_dtype, out_dtypes, tm_target):
    tm = _row_tile(m, min(tm_target, RES_TM_MAX))
    size = lambda dt: jnp.dtype(dt).itemsize
    fixed = 3 * tm * d * 4 + tm * d * 4
    fixed += sum(2 * tm * d * size(dt) for dt in out_dtypes)
    tk = LANES
    for cand in range(LANES, kdim + 1, LANES):
        per_k = 2 * cand * (d * size(w_dtype) + tm * size(a_dtype))
        if kdim % cand == 0 and fixed + per_k <= VMEM_BUDGET:
            tk = cand
    return tm, tk


def mm_res_norm(a, w, w_idx, h, gains, out_dtypes, scale, tm_target):
    m, kdim = a.shape
    d = h.shape[-1]
    tm, tk = _res_norm_tiles(m, kdim, d, a.dtype, w.dtype, out_dtypes, tm_target)
    nk = kdim // tk
    n_norm = gains.shape[0]
    lead = (None,) * len(w_idx)
    row = lambda i, k: (i, 0)
    outs = pl.pallas_call(
        functools.partial(_mm_res_norm_kernel, scale=scale, nk=nk),
        grid=(pl.cdiv(m, tm), nk),
        in_specs=[
            pl.BlockSpec((tm, tk), lambda i, k: (i, k)),
            pl.BlockSpec(lead + (tk, d), lambda i, k: tuple(w_idx) + (k, 0)),
            pl.BlockSpec((tm, d), row, pipeline_mode=pl.Buffered(1)),
            pl.BlockSpec((n_norm, d), lambda i, k: (0, 0)),
        ],
        out_specs=[pl.BlockSpec((tm, d), row)] * (1 + n_norm),
        out_shape=[jax.ShapeDtypeStruct((m, d), F32)]
        + [jax.ShapeDtypeStruct((m, d), dt) for dt in out_dtypes],
        compiler_params=_params("parallel", "arbitrary"),
        name="mm_res_norm",
    )(a, w, h, gains)
    return outs[0], outs[1:]


def _mm_kernel(x_ref, w_ref, o_ref):
    o_ref[...] = _bdot(x_ref[...], w_ref[...]).astype(o_ref.dtype)


def mm(x, w, w_idx, col0, ncols, tm, tn):
    m, kdim = x.shape
    assert col0 % tn == 0 and ncols % tn == 0
    lead = (None,) * len(w_idx)
    c0 = col0 // tn
    return pl.pallas_call(
        _mm_kernel,
        grid=(ncols // tn, pl.cdiv(m, tm)),
        in_specs=[pl.BlockSpec((tm, kdim), lambda j, i: (i, 0)),
                  pl.BlockSpec(lead + (kdim, tn), lambda j, i: tuple(w_idx) + (0, c0 + j))],
        out_specs=pl.BlockSpec((tm, tn), lambda j, i: (i, j)),
        out_shape=jax.ShapeDtypeStruct((m, ncols), F32),
        compiler_params=_params("parallel", "parallel"),
        name="mm",
    )(x, w)


def _gdn_ba_kernel(x_ref, w_ref, alog_ref, dtb_ref, o_ref):
    r = _bdot(x_ref[...], w_ref[...])
    nh = r.shape[1] // 2
    lane = lax.broadcasted_iota(jnp.int32, r.shape, 1)
    xa = r + dtb_ref[...]
    softplus = jnp.maximum(xa, 0.0) + jnp.log1p(jnp.exp(-jnp.abs(xa)))
    g = -jnp.exp(alog_ref[...]) * softplus
    o_ref[...] = jnp.where(lane < nh, jax.nn.sigmoid(r), g)


def gdn_ba(x, w_ba, a_log, dt_bias, tm):
    m, d = x.shape
    nh = a_log.shape[0]
    pad = jnp.zeros((nh,), F32)
    alog = jnp.concatenate([pad, a_log.astype(F32)]).reshape(1, 2 * nh)
    dtb = jnp.concatenate([pad, dt_bias.astype(F32)]).reshape(1, 2 * nh)
    return pl.pallas_call(
        _gdn_ba_kernel,
        grid=(pl.cdiv(m, tm),),
        in_specs=[pl.BlockSpec((tm, d), lambda i: (i, 0)), pl.BlockSpec((d, 2 * nh), lambda i: (0, 0)),
                  pl.BlockSpec((1, 2 * nh), lambda i: (0, 0)), pl.BlockSpec((1, 2 * nh), lambda i: (0, 0))],
        out_specs=pl.BlockSpec((tm, 2 * nh), lambda i: (i, 0)),
        out_shape=jax.ShapeDtypeStruct((m, 2 * nh), F32),
        compiler_params=_params("parallel"),
        name="gdn_ba",
    )(x, w_ba, alog, dtb)


def _l2norm(x):
    return x * lax.rsqrt(jnp.sum(x * x, axis=-1, keepdims=True) + L2_EPS)


def _col(x, idx):
    lane = lax.broadcasted_iota(jnp.int32, x.shape, 1)
    return jnp.sum(jnp.where(lane == idx, x, 0.0), axis=-1, keepdims=True)


def _row_of(col):
    r = col.shape[0]
    return jnp.broadcast_to(col, (r, max(r, 128))).T[:r, :]


def _unit_lower_inverse_minus_eye(a_list, n_sq):
    p = [-a for a in a_list]
    t = list(p)
    for _ in range(n_sq):
        p = [_bdot(x, x) for x in p]
        tp = [_bdot(x, y) for x, y in zip(t, p)]
        t = [x + y + z for x, y, z in zip(t, p, tp)]
    return t


def _delta_wy(kk, qk, k, q, v, beta, gc, tril, strict, n_sq):
    n = len(v)
    decay = [jnp.exp(jnp.where(tril, gc[i] - _row_of(gc[i]), -jnp.inf)) for i in range(n)]
    a = [jnp.where(strict, beta[i] * kk[i] * decay[i], 0.0) for i in range(n)]
    t = _unit_lower_inverse_minus_eye(a, n_sq)
    egc = [jnp.exp(g) for g in gc]
    rhs = [jnp.concatenate([v[i] * beta[i], k[i] * (beta[i] * egc[i])], axis=1) for i in range(n)]
    trhs = [_bdot(t[i], rhs[i]) for i in range(n)]
    return ([rhs[i] + trhs[i] for i in range(n)], [q[i] * egc[i] for i in range(n)],
            [qk[i] * decay[i] for i in range(n)])


def _gdn_prompt_kernel(q_ref, k_ref, v_ref, z_ref, bg_ref, wq_ref, wk_ref, wv_ref, ng_ref,
                       o_ref, s_out_ref, qwin, kwin, vwin, s_ref, *, seq_len, rows, chunk, g_heads, n_sq):
    h = pl.program_id(1)
    c = pl.program_id(2)
    nc = pl.num_programs(2)
    dk = GDN_HEAD
    nb = rows // chunk
    lead = 8

    @pl.when(c == 0)
    def _():
        s_ref[...] = jnp.zeros_like(s_ref)
        for win in (qwin, kwin, vwin):
            win[pl.ds(0, lead), :] = jnp.zeros((lead, win.shape[1]), F32)

    row = lax.broadcasted_iota(jnp.int32, (rows, 1), 0)
    valid = (c * rows + row) < seq_len

    def conv(win, x_ref, w_ref):
        win[pl.ds(lead, rows), :] = x_ref[0]
        acc = win[pl.ds(lead - 3, rows), :] * w_ref[0:1, :]
        for tap in range(1, GDN_CONV_TAPS):
            acc = acc + win[pl.ds(lead - 3 + tap, rows), :] * w_ref[tap:tap + 1, :]
        win[pl.ds(lead - 3, 3), :] = win[pl.ds(lead + rows - 3, 3), :]
        return jnp.where(valid, _silu(acc), 0.0)

    qc = conv(qwin, q_ref, wq_ref)
    kc = conv(kwin, k_ref, wk_ref)
    vc = conv(vwin, v_ref, wv_ref)
    bg = jnp.where(valid, bg_ref[0], 0.0)
    nh = bg.shape[1] // 2

    ri = lax.broadcasted_iota(jnp.int32, (rows, rows), 0)
    ci = lax.broadcasted_iota(jnp.int32, (rows, rows), 1)
    same = (ri // chunk) == (ci // chunk)
    tril = same & (ri >= ci)
    strict = same & (ri > ci)
    gc_all = _exact_left_dot(tril, bg)

    nv = 2 * g_heads
    qs = [_l2norm(qc[:, gi * dk:(gi + 1) * dk]) * (dk ** -0.5) for gi in range(g_heads)]
    ks = [_l2norm(kc[:, gi * dk:(gi + 1) * dk]) for gi in range(g_heads)]
    kks = [_bdot_nt(x, x) for x in ks]
    qks = [_bdot_nt(x, y) for x, y in zip(qs, ks)]
    per_v = lambda xs: [xs[sv // 2] for sv in range(nv)]
    q, k = per_v(qs), per_v(ks)
    v = [vc[:, sv * dk:(sv + 1) * dk] for sv in range(nv)]
    beta = [_col(bg, 2 * h * g_heads + sv) for sv in range(nv)]
    gc = [_col(gc_all, nh + 2 * h * g_heads + sv) for sv in range(nv)]
    uw, qg, qkd = _delta_wy(per_v(kks), per_v(qks), k, q, v, beta, gc, tril, strict, n_sq)
    ow = [_bdot(x, y) for x, y in zip(qkd, uw)]
    q_eff = [qg[i] - ow[i][:, dk:] for i in range(nv)]
    s = [s_ref[sv] for sv in range(nv)]
    o_parts = [[] for _ in range(nv)]
    for j in range(nb):
        sl = slice(j * chunk, (j + 1) * chunk)
        g_end = [g[(j + 1) * chunk - 1:(j + 1) * chunk, :] for g in gc]
        kd = [k[i][sl] * jnp.exp(g_end[i] - gc[i][sl]) for i in range(nv)]
        bn = [_bdot_tn(kd[i], uw[i][sl]) for i in range(nv)]
        qs_j = [_bdot(q_eff[i][sl], s[i]) for i in range(nv)]
        ns_j = [_bdot(bn[i][:, dk:], s[i]) for i in range(nv)]
        for i in range(nv):
            o_parts[i].append(qs_j[i] + ow[i][sl, :dk])
            s[i] = s[i] * jnp.exp(g_end[i]) - ns_j[i] + bn[i][:, :dk]
    outs = []
    for sv in range(nv):
        s_ref[sv] = s[sv]
        o = o_parts[sv][0] if nb == 1 else jnp.concatenate(o_parts[sv], axis=0)
        z = z_ref[0, :, sv * dk:(sv + 1) * dk]
        outs.append(o * _rms_scale(o) * ng_ref[...] * _silu(z))
    o_ref[0] = jnp.concatenate(outs, axis=1).astype(o_ref.dtype)

    @pl.when(c == nc - 1)
    def _():
        s_out_ref[0] = s_ref[...]


def gdn_prompt(raw, bg, conv_w, norm_g, rows, chunk, g_heads):
    b, t, _ = raw.shape
    nh = bg.shape[-1] // 2
    dk = GDN_HEAD
    n_qk = nh // 2
    qk_dim = n_qk * dk
    gq, gv = g_heads * dk, 2 * g_heads * dk
    n_hblk = n_qk // g_heads
    nc = pl.cdiv(t, rows)
    n_sq = int(math.log2(chunk)) - 1
    assert 2 ** (n_sq + 1) == chunk and n_qk % g_heads == 0 and rows % chunk == 0
    blk = lambda width, off: pl.BlockSpec((1, rows, width), lambda bi, h, c: (bi, c, off + h))
    wblk = lambda width, off: pl.BlockSpec((GDN_CONV_TAPS, width), lambda bi, h, c: (0, off + h))
    kernel = functools.partial(_gdn_prompt_kernel, seq_len=t, rows=rows, chunk=chunk, g_heads=g_heads,
                               n_sq=n_sq)
    return pl.pallas_call(
        kernel,
        grid=(b, n_hblk, nc),
        in_specs=[
            blk(gq, 0), blk(gq, n_hblk), blk(gv, n_hblk), blk(gv, 2 * n_hblk),
            pl.BlockSpec((1, rows, 2 * nh), lambda bi, h, c: (bi, c, 0)),
            wblk(gq, 0), wblk(gq, n_hblk), wblk(gv, n_hblk),
            pl.BlockSpec((1, dk), lambda bi, h, c: (0, 0)),
        ],
        out_specs=[
            pl.BlockSpec((1, rows, gv), lambda bi, h, c: (bi, c, h)),
            pl.BlockSpec((1, 2 * g_heads, dk, dk), lambda bi, h, c: (bi, h, 0, 0)),
        ],
        out_shape=[jax.ShapeDtypeStruct((b, t, 2 * qk_dim), BF16),
                   jax.ShapeDtypeStruct((b, nh, dk, dk), F32)],
        scratch_shapes=[pltpu.VMEM((8 + rows, gq), F32), pltpu.VMEM((8 + rows, gq), F32),
                        pltpu.VMEM((8 + rows, gv), F32), pltpu.VMEM((2 * g_heads, dk, dk), F32)],
        compiler_params=_params("parallel", "parallel", "arbitrary"),
        name="gdn_prompt",
    )(raw, raw, raw, raw, bg, conv_w, conv_w, conv_w, norm_g.reshape(1, dk))


def _gdn_sample_kernel(q_ref, k_ref, v_ref, z_ref, bg_ref, cq_ref, ck_ref, cv_ref, wq_ref, wk_ref, wv_ref,
                       ng_ref, s_in_ref, o_ref, s_out_ref, qwin, kwin, vwin, *, n_sq):
    h = pl.program_id(0)
    tb, t, dk = q_ref.shape
    r = tb * t
    lead = 8

    def conv(win, x_ref, c_ref, w_ref):
        win[:, pl.ds(lead - 3, 3), :] = c_ref[...]
        win[:, pl.ds(lead, t), :] = x_ref[...]
        acc = win[:, pl.ds(lead - 3, t), :] * w_ref[0:1, :]
        for i in range(1, 4):
            acc = acc + win[:, pl.ds(lead - 3 + i, t), :] * w_ref[i:i + 1, :]
        return _silu(acc).reshape(r, x_ref.shape[-1])

    qc = conv(qwin, q_ref, cq_ref, wq_ref)
    kc = conv(kwin, k_ref, ck_ref, wk_ref)
    vc = conv(vwin, v_ref, cv_ref, wv_ref)
    bg = bg_ref[...].reshape(r, bg_ref.shape[-1])
    nh = bg.shape[1] // 2

    ri = lax.broadcasted_iota(jnp.int32, (r, r), 0)
    ci = lax.broadcasted_iota(jnp.int32, (r, r), 1)
    same = (ri // t) == (ci // t)
    tril = same & (ri >= ci)
    strict = same & (ri > ci)
    gc_all = _exact_left_dot(tril, bg)
    g_sum = jnp.sum(bg_ref[...], axis=1, keepdims=True)

    q = _l2norm(qc) * (dk ** -0.5)
    k = _l2norm(kc)
    kk = _bdot_nt(k, k)
    qk = _bdot_nt(q, k)
    bdims = (((2,), (1,)), ((0,), (0,)))
    two = range(2)
    v = [vc[:, e * dk:(e + 1) * dk] for e in two]
    beta = [_col(bg, 2 * h + e) for e in two]
    gc = [_col(gc_all, nh + 2 * h + e) for e in two]
    uw, qg, qkd = _delta_wy([kk, kk], [qk, qk], [k, k], [q, q], v, beta, gc, tril, strict, n_sq)
    s = [s_in_ref[:, e] for e in two]
    wq3 = [jnp.concatenate([uw[e][:, dk:].reshape(tb, t, dk), qg[e].reshape(tb, t, dk)], axis=1).astype(BF16)
           for e in two]
    ws_qs = [lax.dot_general(wq3[e], s[e].astype(BF16), bdims, preferred_element_type=F32) for e in two]
    v_new = [uw[e][:, :dk] - ws_qs[e][:, :t].reshape(r, dk) for e in two]
    qkv = [_bdot(qkd[e], v_new[e]) for e in two]
    lane3 = lax.broadcasted_iota(jnp.int32, g_sum.shape, 2)
    g_last = [jnp.sum(jnp.where(lane3 == nh + 2 * h + e, g_sum, 0.0), axis=-1, keepdims=True) for e in two]
    kd = [(k.reshape(tb, t, dk) * jnp.exp(g_last[e] - gc[e].reshape(tb, t, 1))).astype(BF16) for e in two]
    upd = [lax.dot_general(kd[e], v_new[e].reshape(tb, t, dk).astype(BF16), (((1,), (1,)), ((0,), (0,))),
                           preferred_element_type=F32) for e in two]
    for e in two:
        s_out_ref[:, e] = s[e] * jnp.exp(g_last[e]) + upd[e]
        o = ws_qs[e][:, t:].reshape(r, dk) + qkv[e]
        z = z_ref[:, :, e * dk:(e + 1) * dk].reshape(r, dk)
        o_ref[:, :, e * dk:(e + 1) * dk] = (o * _rms_scale(o) * ng_ref[...] * _silu(z)).reshape(tb, t, dk)


def gdn_sample(raw, bg, conv_state, ssm_state, conv_w, norm_g, tb):
    b, t, _ = raw.shape
    nh = bg.shape[-1] // 2
    dk = GDN_HEAD
    n_qk = nh // 2
    n_sq = int(math.log2(t)) - 1
    assert 2 ** (n_sq + 1) == t and b % tb == 0 and t == 8
    blk = lambda width, off: pl.BlockSpec((tb, t, width), lambda h, i: (i, 0, off + h))
    cblk = lambda width, off: pl.BlockSpec((tb, 3, width), lambda h, i: (i, 0, off + h))
    wblk = lambda width, off: pl.BlockSpec((4, width), lambda h, i: (0, off + h))
    sblk = pl.BlockSpec((tb, 2, dk, dk), lambda h, i: (i, h, 0, 0))
    return pl.pallas_call(
        functools.partial(_gdn_sample_kernel, n_sq=n_sq),
        grid=(n_qk, b // tb),
        in_specs=[
            blk(dk, 0), blk(dk, n_qk), blk(2 * dk, n_qk), blk(2 * dk, 2 * n_qk),
            pl.BlockSpec((tb, t, 2 * nh), lambda h, i: (i, 0, 0)),
            cblk(dk, 0), cblk(dk, n_qk), cblk(2 * dk, n_qk),
            wblk(dk, 0), wblk(dk, n_qk), wblk(2 * dk, n_qk),
            pl.BlockSpec((1, dk), lambda h, i: (0, 0)),
            sblk,
        ],
        out_specs=[pl.BlockSpec((tb, t, 2 * dk), lambda h, i: (i, 0, h)), sblk],
        out_shape=[jax.ShapeDtypeStruct((b, t, nh * dk), F32),
                   jax.ShapeDtypeStruct(ssm_state.shape, F32)],
        scratch_shapes=[pltpu.VMEM((tb, 8 + t, dk), F32), pltpu.VMEM((tb, 8 + t, dk), F32),
                        pltpu.VMEM((tb, 8 + t, 2 * dk), F32)],
        compiler_params=_params("parallel", "parallel"),
        name="gdn_sample",
    )(raw, raw, raw, raw, bg, conv_state, conv_state, conv_state, conv_w, conv_w, conv_w,
      norm_g.reshape(1, dk), ssm_state)


def _t5_bucket(dist):
    max_exact = NUM_BUCKETS // 2
    n = jnp.maximum(dist, 0)
    log_ratio = jnp.log(jnp.maximum(n, 1).astype(F32) * (1.0 / max_exact)) * (1.0 / math.log(MAX_DISTANCE / max_exact))
    large = jnp.minimum(max_exact + (log_ratio * (NUM_BUCKETS - max_exact)).astype(jnp.int32), NUM_BUCKETS - 1)
    return jnp.where(n < max_exact, n, large)


def _prompt_bias_kernel(rb_ref, o_ref, *, tq, tk, n_meta):
    ty = pl.program_id(0)
    m = pl.program_id(1)
    ii = lax.broadcasted_iota(jnp.int32, (tq, tk), 0)
    jj = lax.broadcasted_iota(jnp.int32, (tq, tk), 1)
    dist = n_meta + (ty - 1) * tk + ii - jj
    bucket = _t5_bucket(dist)
    tile = jnp.full((tq, tk), NEG_BIG, F32)
    for bkt in range(NUM_BUCKETS):
        tile = jnp.where((bucket == bkt) & (dist >= 0), rb_ref[bkt, m], tile)
    o_ref[0, 0] = tile


def prompt_bias_tiles(rel_bias, tq, tk, n_meta):
    n_maps = rel_bias.shape[1]
    return pl.pallas_call(
        functools.partial(_prompt_bias_kernel, tq=tq, tk=tk, n_meta=n_meta),
        grid=(3, n_maps),
        in_specs=[pl.BlockSpec(memory_space=pltpu.SMEM)],
        out_specs=pl.BlockSpec((1, 1, tq, tk), lambda ty, m: (ty, m, 0, 0)),
        out_shape=jax.ShapeDtypeStruct((3, n_maps, tq, tk), F32),
        compiler_params=_params("parallel", "parallel"),
        name="prompt_bias",
    )(rel_bias.astype(F32))


def _lambda_of(lv_ref, lam_init):
    lv = lv_ref[...].astype(F32)
    a = jnp.sum(lv[0:1] * lv[1:2], axis=-1, keepdims=True)
    b = jnp.sum(lv[2:3] * lv[3:4], axis=-1, keepdims=True)
    return jnp.exp(a) - jnp.exp(b) + lam_init


def _attn_prompt_kernel(qi_ref, kj_ref, q_ref, k_ref, v_ref, bias_ref, rb_ref, lv_ref, sub_ref, o_ref,
                        m_ref, l_ref, acc_ref, *, n_keys, nk, tk, hp, lam_init, scale):
    h = pl.program_id(1)
    step = pl.program_id(2)
    qi = qi_ref[step]
    kj = kj_ref[step]
    dh = DIFF_HEAD
    hw = 2 * dh

    @pl.when(kj == 0)
    def _():
        m_ref[...] = jnp.full_like(m_ref, NEG_BIG)
        l_ref[...] = jnp.zeros_like(l_ref)
        acc_ref[...] = jnp.zeros_like(acc_ref)

    krow = lax.broadcasted_iota(jnp.int32, (tk, 1), 0)
    kvalid = (kj * tk + krow) < n_keys
    k = jnp.where(kvalid, k_ref[0], 0.0).astype(BF16)
    v = jnp.where(kvalid, v_ref[0], 0.0).astype(BF16)
    near = (qi - kj) <= 1
    lanes = m_ref.shape[-1]
    wide = lambda x, n: jnp.concatenate([x] * (n // lanes), axis=1)
    ones = jnp.ones((tk, lanes), BF16)
    maps = range(2 * hp)
    s = [_bdot_nt((q_ref[0, :, mi * dh:(mi + 1) * dh] * scale).astype(BF16), k[:, mi * dh:(mi + 1) * dh])
         for mi in maps]
    ps, corrs = [], []
    for mi in maps:
        far_bias = rb_ref[NUM_BUCKETS - 1, 2 * hp * h + mi]
        sb = s[mi] + jnp.where(near, bias_ref[0, mi], far_bias)
        m_old = m_ref[mi]
        m_new = jnp.maximum(m_old, jnp.max(sb, axis=-1, keepdims=True))
        m_ref[mi] = m_new
        corrs.append(jnp.exp(m_old - m_new))
        ps.append(jnp.exp(sb - wide(m_new, tk)).astype(BF16))
    sums = [jnp.dot(ps[mi], ones, preferred_element_type=F32) for mi in maps]
    pvs = [jnp.dot(ps[mi], v[:, (mi // 2) * hw:(mi // 2 + 1) * hw], preferred_element_type=F32) for mi in maps]
    for mi in maps:
        l_ref[mi] = l_ref[mi] * corrs[mi] + sums[mi]
        acc_ref[mi] = acc_ref[mi] * wide(corrs[mi], hw) + pvs[mi]

    @pl.when(kj == jnp.minimum(qi + 1, nk - 1))
    def _():
        lam = _lambda_of(lv_ref, lam_init)
        for hh in range(hp):
            o = (acc_ref[2 * hh] / wide(l_ref[2 * hh], hw)
                 - lam * (acc_ref[2 * hh + 1] / wide(l_ref[2 * hh + 1], hw)))
            o_ref[0, :, hh * hw:(hh + 1) * hw] = (
                o * _rms_scale(o) * sub_ref[...] * (1.0 - lam_init)).astype(o_ref.dtype)


def attn_prompt(q, k, v, bias_tiles, rel_bias, lam_vecs, subln, lam_init, tq, tk, hp):
    b, s, d = q.shape
    n_keys = k.shape[1]
    n_heads = d // (2 * DIFF_HEAD)
    hw = 2 * DIFF_HEAD
    bw = hp * hw
    nq, nk = s // tq, pl.cdiv(n_keys, tk)
    assert s % tq == 0 and tq == tk and N_META <= tk and n_heads % hp == 0
    assert N_META + 2 * tk - (tk - 1) >= MAX_DISTANCE
    pairs = [(qi, kj) for qi in range(nq) for kj in range(min(qi + 1, nk - 1) + 1)]
    qi_tab = jnp.asarray([p[0] for p in pairs], jnp.int32)
    kj_tab = jnp.asarray([p[1] for p in pairs], jnp.int32)

    qmap = lambda bi, h, st, qt, kt: (bi, qt[st], h)
    kvmap = lambda bi, h, st, qt, kt: (bi, kt[st], h)
    bias_map = lambda bi, h, st, qt, kt: (jnp.clip(qt[st] - kt[st] + 1, 0, 2), h, 0, 0)
    const2 = lambda bi, h, st, qt, kt: (0, 0)
    kernel = functools.partial(_attn_prompt_kernel, n_keys=n_keys, nk=nk, tk=tk, hp=hp, lam_init=lam_init,
                               scale=DIFF_HEAD ** -0.5)
    grid_spec = pltpu.PrefetchScalarGridSpec(
        num_scalar_prefetch=2,
        grid=(b, n_heads // hp, len(pairs)),
        in_specs=[
            pl.BlockSpec((1, tq, bw), qmap),
            pl.BlockSpec((1, tk, bw), kvmap),
            pl.BlockSpec((1, tk, bw), kvmap),
            pl.BlockSpec((1, 2 * hp, tq, tk), bias_map),
            pl.BlockSpec(memory_space=pltpu.SMEM),
            pl.BlockSpec(lam_vecs.shape, const2),
            pl.BlockSpec((1, hw), const2),
        ],
        out_specs=pl.BlockSpec((1, tq, bw), qmap),
        scratch_shapes=[pltpu.VMEM((2 * hp, tq, LANES), F32), pltpu.VMEM((2 * hp, tq, LANES), F32),
                        pltpu.VMEM((2 * hp, tq, hw), F32)],
    )
    return pl.pallas_call(
        kernel,
        grid_spec=grid_spec,
        out_shape=jax.ShapeDtypeStruct((b, s, d), BF16),
        compiler_params=_params("parallel", "parallel", "arbitrary"),
        name="attn_prompt",
    )(qi_tab, kj_tab, q, k, v, bias_tiles, rel_bias.astype(F32), lam_vecs, subln.reshape(1, hw))


def _sample_bias_kernel(rb_ref, o_ref, *, t, page, past_len):
    r = o_ref.shape[1]
    ri = lax.broadcasted_iota(jnp.int32, (r, page), 0)
    jj = lax.broadcasted_iota(jnp.int32, (r, page), 1)
    tq = ri % t
    far = rb_ref[:, NUM_BUCKETS - 1:NUM_BUCKETS]
    o_ref[0] = jnp.broadcast_to(far, (r, page))
    for ty, dist, ok in ((1, page + tq - jj, None), (2, tq - jj, (jj <= tq) & (jj < t))):
        bucket = _t5_bucket(dist)
        tile = jnp.full((r, page), NEG_BIG, F32)
        for bkt in range(NUM_BUCKETS):
            hit = (bucket == bkt) if ok is None else ((bucket == bkt) & ok)
            tile = jnp.where(hit, rb_ref[:, bkt:bkt + 1], tile)
        o_ref[ty] = tile
    del past_len


def sample_bias_tiles(rel_bias, t, page, past_len):
    n_maps = rel_bias.shape[1]
    assert past_len % page == 0 and past_len - page >= MAX_DISTANCE
    rb_rows = jnp.repeat(rel_bias.astype(F32).T, t, axis=0)
    return pl.pallas_call(
        functools.partial(_sample_bias_kernel, t=t, page=page, past_len=past_len),
        out_shape=jax.ShapeDtypeStruct((3, n_maps * t, page), F32),
        compiler_params=pltpu.CompilerParams(vmem_limit_bytes=VMEM_LIMIT),
        name="sample_bias",
    )(rb_rows)


def _attn_sample_kernel(*refs, lam_init, scale, page, pps):
    pt_ref, q_ref = refs[0], refs[1]
    kc_refs, vlo_refs, vhi_refs = refs[2:2 + pps], refs[2 + pps:2 + 2 * pps], refs[2 + 2 * pps:2 + 3 * pps]
    (kn_ref, vn_ref, far_ref, bias_ref, nbias_ref, lv_ref, sub_ref, o_ref, m_ref, l_ref, acc_ref) = refs[2 + 3 * pps:]
    del pt_ref
    p = pl.program_id(1)
    n_steps = pl.num_programs(1)
    t, d = q_ref.shape[1], q_ref.shape[2]
    dh = DIFF_HEAD
    hw = 2 * dh
    n_maps = d // dh
    n_heads = n_maps // 2

    @pl.when(p == 0)
    def _():
        m_ref[...] = jnp.full_like(m_ref, NEG_BIG)
        l_ref[...] = jnp.zeros_like(l_ref)
        acc_ref[...] = jnp.zeros_like(acc_ref)

    def update(k_ofs, v_ofs, biases):
        s = jnp.concatenate(
            [jnp.concatenate([_bdot_nt(q_ref[0, :, m * dh:(m + 1) * dh] * scale, k_of(m)) for m in range(n_maps)],
                             axis=0) + bias for k_of, bias in zip(k_ofs, biases)], axis=1)
        m_old = m_ref[...]
        m_new = jnp.maximum(m_old, jnp.max(s, axis=-1, keepdims=True))
        corr = jnp.exp(m_old - m_new)
        pr = jnp.exp(s - m_new)
        l_ref[...] = l_ref[...] * corr + jnp.sum(pr, axis=-1, keepdims=True)
        pv = jnp.concatenate(
            [_bdot(pr[2 * h * t:(2 * h + 2) * t], jnp.concatenate([v_of(h) for v_of in v_ofs], axis=0))
             for h in range(n_heads)], axis=0)
        acc_ref[...] = acc_ref[...] * corr + pv
        m_ref[...] = m_new

    strided_k = lambda ref: (lambda m: ref[pl.ds(m, page, stride=n_maps), :])
    strided_v = lambda lo, hi: (lambda h: jnp.concatenate([lo[pl.ds(h, page, stride=n_heads), :],
                                                           hi[pl.ds(h, page, stride=n_heads), :]], axis=1))
    update([strided_k(r) for r in kc_refs], [strided_v(lo, hi) for lo, hi in zip(vlo_refs, vhi_refs)],
           [far_ref[0]] * (pps - 1) + [bias_ref[0]])

    @pl.when(p == n_steps - 1)
    def _():
        kn, vn = kn_ref[0], vn_ref[0]
        zk = jnp.zeros((page - t, dh), F32)
        zv = jnp.zeros((page - t, hw), F32)
        update([lambda m: jnp.concatenate([kn[:, m * dh:(m + 1) * dh], zk], axis=0)],
               [lambda h: jnp.concatenate([vn[:, h * hw:(h + 1) * hw], zv], axis=0)], [nbias_ref[0]])
        lam = _lambda_of(lv_ref, lam_init)
        for h in range(n_heads):
            r0 = 2 * h * t
            o0 = acc_ref[pl.ds(r0, t), :] / l_ref[pl.ds(r0, t), :]
            o1 = acc_ref[pl.ds(r0 + t, t), :] / l_ref[pl.ds(r0 + t, t), :]
            o = o0 - lam * o1
            o_ref[0, :, h * hw:(h + 1) * hw] = o * _rms_scale(o) * sub_ref[...] * (1.0 - lam_init)


def attn_sample(q, k_new, v_new, cache_k, cache_v, page_table, bias_tiles, lam_vecs, subln, lam_init, pps):
    b, t, d = q.shape
    n_pages = page_table.shape[1]
    dh = DIFF_HEAD
    hw = 2 * dh
    n_maps = d // dh
    page = cache_k.shape[1] // n_maps
    r = n_maps * t
    n_steps = n_pages // pps
    assert r == bias_tiles.shape[1] and page == bias_tiles.shape[2] and cache_v.shape[1] == page * n_maps // 2
    assert n_pages % pps == 0

    def cache_map(i, half):
        return lambda bi, p, pt: (pt[bi, p * pps + i], 0, half)

    row = lambda bi, p, pt: (bi, 0, 0)
    tile = lambda index: pl.BlockSpec((1, r, page), index)
    grid_spec = pltpu.PrefetchScalarGridSpec(
        num_scalar_prefetch=1,
        grid=(b, n_steps),
        in_specs=[pl.BlockSpec((1, t, d), row)]
        + [pl.BlockSpec((None, page * n_maps, dh), cache_map(i, 0)) for i in range(pps)]
        + [pl.BlockSpec((None, page * n_maps // 2, dh), cache_map(i, 0)) for i in range(pps)]
        + [pl.BlockSpec((None, page * n_maps // 2, dh), cache_map(i, 1)) for i in range(pps)]
        + [
            pl.BlockSpec((1, t, d), row),
            pl.BlockSpec((1, t, d), row),
            tile(lambda bi, p, pt: (0, 0, 0)),
            tile(lambda bi, p, pt: (jnp.where(p == n_steps - 1, 1, 0), 0, 0)),
            tile(lambda bi, p, pt: (2, 0, 0)),
            pl.BlockSpec(lam_vecs.shape, lambda bi, p, pt: (0, 0)),
            pl.BlockSpec((1, hw), lambda bi, p, pt: (0, 0)),
        ],
        out_specs=pl.BlockSpec((1, t, d), row),
        scratch_shapes=[pltpu.VMEM((r, 1), F32), pltpu.VMEM((r, 1), F32), pltpu.VMEM((r, hw), F32)],
    )
    return pl.pallas_call(
        functools.partial(_attn_sample_kernel, lam_init=lam_init, scale=dh ** -0.5, page=page, pps=pps),
        grid_spec=grid_spec,
        out_shape=jax.ShapeDtypeStruct((b, t, d), F32),
        compiler_params=_params("parallel", "arbitrary"),
        name="attn_sample",
    )(page_table, q, *([cache_k] * pps), *([cache_v] * (2 * pps)), k_new, v_new, bias_tiles, bias_tiles,
      bias_tiles, lam_vecs, subln.reshape(1, hw))


TN = 512
TN_MM = 1024
ATTN_TILE = 256
ATTN_HEADS_PER_STEP = 4
GDN_CHUNK = 64
GDN_ROWS = 128
GDN_HEADS_PER_STEP = 4
GDN_SEQS_PER_STEP = 16
PAGES_PER_STEP = 4


def _row_tile(m, target):
    best = 16
    for t in range(16, target + 1, 16):
        if m % t == 0:
            best = t
    assert m % best == 0
    return best


def _trunk(h, n_seq, n_drop, gdn_fn, attn_fn, kv_fn, ln_gain, ffn_w_in, ffn_w_out, gdn_w_out, kv_norm,
           diff_w_q, diff_w_out, final_norm, n_a, tm_target):
    depth = ln_gain.shape[0]
    d = h.shape[-1]
    tm = _row_tile(h.shape[0], tm_target)
    hn = rmsnorm(h, ln_gain[0, 0], tm)
    conv_states, ssm_states, kv = [], [], None
    y = None
    for i in range(depth):
        tm = _row_tile(h.shape[0], tm_target)
        act = ffn_in(hn, ffn_w_in, i, 0, tm, TN)
        h, (hn,) = mm_res_norm(act, ffn_w_out, (i, 0), h, ln_gain[i, 1][None], [BF16], 0.5, tm_target)
        if i < n_a:
            mix, conv_new, ssm_new = gdn_fn(i, hn, tm)
            conv_states.append(conv_new)
            ssm_states.append(ssm_new)
            h, (hn,) = mm_res_norm(mix, gdn_w_out, (i,), h, ln_gain[i, 2][None], [BF16], 1.0, tm_target)
        else:
            j = i - n_a
            q = mm(hn, diff_w_q, (j,), 0, d, tm, TN_MM)
            att = attn_fn(j, i, q, kv)
            h, (hn,) = mm_res_norm(att, diff_w_out, (j,), h, ln_gain[i, 2][None], [BF16], 1.0, tm_target)
        act = ffn_in(hn, ffn_w_in, i, 1, tm, TN)
        if i == depth - 1:
            _, (y,) = mm_res_norm(act, ffn_w_out, (i, 1), h, final_norm[None], [F32], 0.5, tm_target)
        elif i == n_a - 1:
            gains = jnp.stack([kv_norm, ln_gain[i + 1, 0]])
            h, (hkv, hn) = mm_res_norm(act, ffn_w_out, (i, 1), h, gains, [BF16, BF16], 0.5, tm_target)
            kv = kv_fn(hkv, tm)
            if n_drop:
                t = h.shape[0] // n_seq
                drop = lambda x: x.reshape(n_seq, t, d)[:, n_drop:].reshape(n_seq * (t - n_drop), d)
                h, hn = drop(h), drop(hn)
        else:
            h, (hn,) = mm_res_norm(act, ffn_w_out, (i, 1), h, ln_gain[i + 1, 0][None], [BF16], 0.5, tm_target)
    return y, kv, jnp.stack(conv_states), jnp.stack(ssm_states)


def kernel(x_prompt, x_sample, cache_k, cache_v, state_conv, state_ssm, page_table, meta_tokens, ln_gain,
           ffn_w_in, ffn_w_out, gdn_w_in, gdn_conv, gdn_a_log, gdn_dt_bias, gdn_norm, gdn_w_out, kv_norm,
           w_kv, diff_w_q, diff_lambda, diff_subln, diff_w_out, rel_bias, final_norm):
    b, seq, d = x_prompt.shape
    db, dseq, _ = x_sample.shape
    n_a = gdn_w_in.shape[0]
    n_heads_v = gdn_a_log.shape[1]
    qk_dim = (n_heads_v // 2) * GDN_HEAD
    conv_dim = gdn_conv.shape[-1]
    main_cols = conv_dim + n_heads_v * GDN_HEAD
    n_meta = meta_tokens.shape[0]
    t_prompt = n_meta + seq
    kv_cols = w_kv.shape[1] // 2
    n_maps = rel_bias.shape[1]
    page = cache_k.shape[1]
    past_len = page_table.shape[1] * page
    assert n_meta == N_META and conv_dim == 2 * qk_dim + n_heads_v * GDN_HEAD
    lam_init = lambda layer: 0.8 - 0.6 * math.exp(-0.3 * layer)

    def gdn_proj(i, hn, tm):
        raw = mm(hn, gdn_w_in, (i,), 0, main_cols, tm, TN_MM)
        bg = gdn_ba(hn, gdn_w_in[i][:, main_cols:], gdn_a_log[i], gdn_dt_bias[i], tm)
        return raw, bg

    def kv_proj(hkv, tm):
        return (mm(hkv, w_kv, (), 0, kv_cols, tm, TN_MM), mm(hkv, w_kv, (), kv_cols, kv_cols, tm, TN_MM))

    def gdn_prompt_fn(i, hn, tm):
        raw, bg = gdn_proj(i, hn, tm)
        raw3 = raw.reshape(b, t_prompt, main_cols)
        o, ssm = gdn_prompt(raw3, bg.reshape(b, t_prompt, -1), gdn_conv[i], gdn_norm[i], GDN_ROWS, GDN_CHUNK,
                            GDN_HEADS_PER_STEP)
        return o.reshape(b * t_prompt, -1), raw3[:, t_prompt - GDN_CONV_TAPS + 1:, :conv_dim], ssm

    prompt_bias = prompt_bias_tiles(rel_bias, ATTN_TILE, ATTN_TILE, n_meta)

    def attn_prompt_fn(j, layer, q, kv):
        k, v = kv
        o = attn_prompt(q.reshape(b, seq, d), k.reshape(b, t_prompt, kv_cols), v.reshape(b, t_prompt, kv_cols),
                        prompt_bias, rel_bias, diff_lambda[j], diff_subln[j], lam_init(layer), ATTN_TILE, ATTN_TILE,
                        ATTN_HEADS_PER_STEP)
        return o.reshape(b * seq, d)

    meta = jnp.broadcast_to(meta_tokens.astype(x_prompt.dtype)[None], (b, n_meta, d))
    h_prompt = jnp.concatenate([meta, x_prompt], axis=1).reshape(b * t_prompt, d)
    shared = (ln_gain, ffn_w_in, ffn_w_out.astype(BF16), gdn_w_out.astype(BF16), kv_norm, diff_w_q,
              diff_w_out.astype(BF16), final_norm, n_a)
    y_p, (k_p, v_p), conv_p, ssm_p = _trunk(h_prompt, b, n_meta, gdn_prompt_fn, attn_prompt_fn, kv_proj,
                                            *shared, tm_target=768)

    def gdn_sample_fn(i, hn, tm):
        raw, bg = gdn_proj(i, hn, tm)
        raw3 = raw.reshape(db, dseq, main_cols)
        o, ssm = gdn_sample(raw3, bg.reshape(db, dseq, -1), state_conv[i], state_ssm[i], gdn_conv[i],
                            gdn_norm[i], GDN_SEQS_PER_STEP)
        return o.reshape(db * dseq, -1), raw3[:, dseq - 3:, :conv_dim], ssm

    sample_bias = sample_bias_tiles(rel_bias, dseq, page, past_len)
    cache_k2 = cache_k.reshape(cache_k.shape[0], page * cache_k.shape[2], cache_k.shape[3])
    cache_v2 = cache_v.reshape(cache_v.shape[0], page * cache_v.shape[2], cache_v.shape[3])

    def attn_sample_fn(j, layer, q, kv):
        k, v = kv
        r3 = lambda x: x.reshape(db, dseq, kv_cols)
        o = attn_sample(r3(q), r3(k), r3(v), cache_k2, cache_v2, page_table, sample_bias, diff_lambda[j],
                        diff_subln[j], lam_init(layer), PAGES_PER_STEP)
        return o.reshape(db * dseq, d)

    y_s, (k_s, v_s), conv_s, ssm_s = _trunk(x_sample.reshape(db * dseq, d), db, 0, gdn_sample_fn,
                                            attn_sample_fn, kv_proj, *shared, tm_target=1024)

    hd = DIFF_HEAD
    return (y_p.reshape(b, seq, d), y_s.reshape(db, dseq, d),
            k_p.reshape(b, t_prompt, n_maps, hd), v_p.reshape(b, t_prompt, n_maps // 2, 2 * hd),
            conv_p, ssm_p,
            k_s.reshape(db, dseq, n_maps, hd), v_s.reshape(db, dseq, n_maps // 2, 2 * hd),
            conv_s, ssm_s)
```

```python
import functools
import math

import jax
import jax.numpy as jnp
from jax import lax
from jax.experimental import pallas as pl
from jax.experimental.pallas import tpu as pltpu

F32 = jnp.float32
BF16 = jnp.bfloat16

EPS = 1e-6
L2_EPS = 1e-6
N_META = 16
GDN_HEAD = 128
GDN_CONV_TAPS = 4
DIFF_HEAD = 128
NUM_BUCKETS = 32
MAX_DISTANCE = 128
NEG_BIG = -1e30
LANES = 128
VMEM_LIMIT = 56 * 1024 * 1024
VMEM_BUDGET = 44 * 1024 * 1024
RES_TM_MAX = 768


def _params(*sem):
    return pltpu.CompilerParams(dimension_semantics=sem, vmem_limit_bytes=VMEM_LIMIT)


def _bdot(a, b):
    return jnp.dot(a.astype(BF16), b.astype(BF16), preferred_element_type=F32)


def _bdot_nt(a, b):
    return lax.dot_general(a.astype(BF16), b.astype(BF16), (((1,), (1,)), ((), ())),
                           preferred_element_type=F32)


def _bdot_tn(a, b):
    return lax.dot_general(a.astype(BF16), b.astype(BF16), (((0,), (0,)), ((), ())),
                           preferred_element_type=F32)


def _hdot(a, b):
    return jnp.dot(a, b, precision=lax.Precision.HIGHEST, preferred_element_type=F32)


def _split3(x):
    hi = x.astype(BF16)
    r1 = x - hi.astype(F32)
    mid = r1.astype(BF16)
    lo = (r1 - mid.astype(F32)).astype(BF16)
    return hi, mid, lo


def _exact_left_dot(mask01, x):
    m = mask01.astype(BF16)
    hi, mid, lo = _split3(x)
    d = functools.partial(jnp.dot, preferred_element_type=F32)
    return d(m, hi) + d(m, mid) + d(m, lo)


def _silu(x):
    return x * jax.nn.sigmoid(x)


def _rms_scale(x):
    return lax.rsqrt(jnp.mean(x * x, axis=-1, keepdims=True) + EPS)


def _rmsnorm_kernel(x_ref, g_ref, o_ref):
    x = x_ref[...].astype(F32)
    o_ref[...] = (x * _rms_scale(x) * g_ref[...]).astype(o_ref.dtype)


def rmsnorm(x, gain, tm):
    m, d = x.shape
    return pl.pallas_call(
        _rmsnorm_kernel,
        grid=(pl.cdiv(m, tm),),
        in_specs=[pl.BlockSpec((tm, d), lambda i: (i, 0)), pl.BlockSpec((1, d), lambda i: (0, 0))],
        out_specs=pl.BlockSpec((tm, d), lambda i: (i, 0)),
        out_shape=jax.ShapeDtypeStruct((m, d), BF16),
        compiler_params=_params("parallel"),
        name="rmsnorm",
    )(x, gain.reshape(1, d))


def _row_tile_spec(x, tm, row0, tile_of, **kwargs):
    if x.ndim == 2:
        return pl.BlockSpec((tm, x.shape[1]), lambda *g: (tile_of(*g), 0), **kwargs), x.shape[0]
    n_seq, t_full, d = x.shape
    t = t_full - row0
    tiles_per_seq = t // tm
    assert t % tm == 0 and row0 % 16 == 0

    def index(*g):
        i = tile_of(*g)
        return (i // tiles_per_seq, pl.multiple_of(row0 + (i % tiles_per_seq) * tm, 16), 0)

    return pl.BlockSpec((pl.Element(1), pl.Element(tm), pl.Element(d)), index, **kwargs), n_seq * t


def _ffn_in_kernel(x_ref, wg_ref, wu_ref, o_ref):
    x = x_ref[...].reshape(x_ref.shape[-2:])
    g = jnp.dot(x, wg_ref[...].astype(BF16), preferred_element_type=F32)
    u = jnp.dot(x, wu_ref[...].astype(BF16), preferred_element_type=F32)
    o_ref[...] = (_silu(g) * u).astype(o_ref.dtype)


def ffn_in(x, w_in, layer, which, tm, tn, row0=0):
    d = x.shape[-1]
    d_ff = w_in.shape[-1] // 2
    ncol = d_ff // tn
    assert d_ff % tn == 0
    xspec, m = _row_tile_spec(x, tm, row0, lambda j, i: i)
    wspec = lambda off: pl.BlockSpec((None, None, d, tn), lambda j, i: (layer, which, 0, j + off))
    return pl.pallas_call(
        _ffn_in_kernel,
        grid=(ncol, pl.cdiv(m, tm)),
        in_specs=[xspec, wspec(0), wspec(ncol)],
        out_specs=pl.BlockSpec((tm, tn), lambda j, i: (i, j)),
        out_shape=jax.ShapeDtypeStruct((m, d_ff), BF16),
        compiler_params=_params("parallel", "parallel"),
        name="ffn_in",
    )(x, w_in, w_in)


def _mm_res_norm_kernel(a_ref, w_ref, h_ref, g_ref, oh_ref, *on_refs, scale, nk):
    k = pl.program_id(1)

    @pl.when(k == 0)
    def _():
        oh_ref[...] = h_ref[...].reshape(oh_ref.shape)

    oh_ref[...] += scale * _bdot(a_ref[...], w_ref[...])

    @pl.when(k == nk - 1)
    def _():
        h = oh_ref[...]
        hs = h * _rms_scale(h)
        for n, on in enumerate(on_refs):
            on[...] = (hs * g_ref[n:n + 1, :]).astype(on.dtype)


def _res_norm_tiles(m, kdim, d, a_dtype, w_dtype, out_dtypes, tm_target):
    tm = _row_tile(m, min(tm_target, RES_TM_MAX))
    size = lambda dt: jnp.dtype(dt).itemsize
    fixed = 3 * tm * d * 4 + tm * d * 4
    fixed += sum(2 * tm * d * size(dt) for dt in out_dtypes)
    tk = LANES
    for cand in range(LANES, kdim + 1, LANES):
        per_k = 2 * cand * (d * size(w_dtype) + tm * size(a_dtype))
        if kdim % cand == 0 and fixed + per_k <= VMEM_BUDGET:
            tk = cand
    return tm, tk


def mm_res_norm(a, w, w_idx, h, gains, out_dtypes, scale, tm_target, row0=0):
    m, kdim = a.shape
    d = h.shape[-1]
    tm, tk = _res_norm_tiles(m, kdim, d, a.dtype, w.dtype, out_dtypes, tm_target)
    nk = kdim // tk
    n_norm = gains.shape[0]
    lead = (None,) * len(w_idx)
    row = lambda i, k: (i, 0)
    hspec, h_rows = _row_tile_spec(h, tm, row0, lambda i, k: i, pipeline_mode=pl.Buffered(1))
    assert h_rows == m
    outs = pl.pallas_call(
        functools.partial(_mm_res_norm_kernel, scale=scale, nk=nk),
        grid=(pl.cdiv(m, tm), nk),
        in_specs=[
            pl.BlockSpec((tm, tk), lambda i, k: (i, k)),
            pl.BlockSpec(lead + (tk, d), lambda i, k: tuple(w_idx) + (k, 0)),
            hspec,
            pl.BlockSpec((n_norm, d), lambda i, k: (0, 0)),
        ],
        out_specs=[pl.BlockSpec((tm, d), row)] * (1 + n_norm),
        out_shape=[jax.ShapeDtypeStruct((m, d), F32)]
        + [jax.ShapeDtypeStruct((m, d), dt) for dt in out_dtypes],
        compiler_params=_params("parallel", "arbitrary"),
        name="mm_res_norm",
    )(a, w, h, gains)
    return outs[0], outs[1:]


def _mm_kernel(x_ref, w_ref, o_ref):
    o_ref[...] = _bdot(x_ref[...], w_ref[...]).astype(o_ref.dtype)


def mm(x, w, w_idx, col0, ncols, tm, tn):
    m, kdim = x.shape
    assert col0 % tn == 0 and ncols % tn == 0
    lead = (None,) * len(w_idx)
    c0 = col0 // tn
    return pl.pallas_call(
        _mm_kernel,
        grid=(ncols // tn, pl.cdiv(m, tm)),
        in_specs=[pl.BlockSpec((tm, kdim), lambda j, i: (i, 0)),
                  pl.BlockSpec(lead + (kdim, tn), lambda j, i: tuple(w_idx) + (0, c0 + j))],
        out_specs=pl.BlockSpec((tm, tn), lambda j, i: (i, j)),
        out_shape=jax.ShapeDtypeStruct((m, ncols), F32),
        compiler_params=_params("parallel", "parallel"),
        name="mm",
    )(x, w)


def _gdn_ba_kernel(x_ref, w_ref, alog_ref, dtb_ref, o_ref):
    r = _bdot(x_ref[...], w_ref[...])
    nh = r.shape[1] // 2
    lane = lax.broadcasted_iota(jnp.int32, r.shape, 1)
    xa = r + dtb_ref[...]
    softplus = jnp.maximum(xa, 0.0) + jnp.log1p(jnp.exp(-jnp.abs(xa)))
    g = -jnp.exp(alog_ref[...]) * softplus
    o_ref[...] = jnp.where(lane < nh, jax.nn.sigmoid(r), g)


def gdn_ba(x, w_ba, a_log, dt_bias, tm):
    m, d = x.shape
    nh = a_log.shape[0]
    pad = jnp.zeros((nh,), F32)
    alog = jnp.concatenate([pad, a_log.astype(F32)]).reshape(1, 2 * nh)
    dtb = jnp.concatenate([pad, dt_bias.astype(F32)]).reshape(1, 2 * nh)
    return pl.pallas_call(
        _gdn_ba_kernel,
        grid=(pl.cdiv(m, tm),),
        in_specs=[pl.BlockSpec((tm, d), lambda i: (i, 0)), pl.BlockSpec((d, 2 * nh), lambda i: (0, 0)),
                  pl.BlockSpec((1, 2 * nh), lambda i: (0, 0)), pl.BlockSpec((1, 2 * nh), lambda i: (0, 0))],
        out_specs=pl.BlockSpec((tm, 2 * nh), lambda i: (i, 0)),
        out_shape=jax.ShapeDtypeStruct((m, 2 * nh), F32),
        compiler_params=_params("parallel"),
        name="gdn_ba",
    )(x, w_ba, alog, dtb)


def _l2norm(x):
    return x * lax.rsqrt(jnp.sum(x * x, axis=-1, keepdims=True) + L2_EPS)


def _col(x, idx):
    lane = lax.broadcasted_iota(jnp.int32, x.shape, 1)
    return jnp.sum(jnp.where(lane == idx, x, 0.0), axis=-1, keepdims=True)


def _row_of(col):
    r = col.shape[0]
    return jnp.broadcast_to(col, (r, max(r, 128))).T[:r, :]


def _unit_lower_inverse_minus_eye(a_list, n_sq):
    p = [-a for a in a_list]
    t = list(p)
    for _ in range(n_sq):
        p = [_bdot(x, x) for x in p]
        tp = [_bdot(x, y) for x, y in zip(t, p)]
        t = [x + y + z for x, y, z in zip(t, p, tp)]
    return t


def _delta_wy(kk, qk, k, q, v, beta, gc, tril, strict, n_sq):
    n = len(v)
    decay = [jnp.exp(jnp.where(tril, gc[i] - _row_of(gc[i]), -jnp.inf)) for i in range(n)]
    a = [jnp.where(strict, beta[i] * kk[i] * decay[i], 0.0) for i in range(n)]
    t = _unit_lower_inverse_minus_eye(a, n_sq)
    egc = [jnp.exp(g) for g in gc]
    rhs = [jnp.concatenate([v[i] * beta[i], k[i] * (beta[i] * egc[i])], axis=1) for i in range(n)]
    trhs = [_bdot(t[i], rhs[i]) for i in range(n)]
    return ([rhs[i] + trhs[i] for i in range(n)], [q[i] * egc[i] for i in range(n)],
            [qk[i] * decay[i] for i in range(n)])


def _gdn_prompt_kernel(q_ref, k_ref, v_ref, z_ref, bg_ref, wq_ref, wk_ref, wv_ref, ng_ref,
                       o_ref, s_out_ref, qwin, kwin, vwin, s_ref, *, seq_len, rows, chunk, g_heads, n_sq):
    h = pl.program_id(1)
    c = pl.program_id(2)
    nc = pl.num_programs(2)
    dk = GDN_HEAD
    nb = rows // chunk
    lead = 8

    @pl.when(c == 0)
    def _():
        s_ref[...] = jnp.zeros_like(s_ref)
        for win in (qwin, kwin, vwin):
            win[pl.ds(0, lead), :] = jnp.zeros((lead, win.shape[1]), F32)

    row = lax.broadcasted_iota(jnp.int32, (rows, 1), 0)
    valid = (c * rows + row) < seq_len

    def conv(win, x_ref, w_ref):
        win[pl.ds(lead, rows), :] = x_ref[0]
        acc = win[pl.ds(lead - 3, rows), :] * w_ref[0:1, :]
        for tap in range(1, GDN_CONV_TAPS):
            acc = acc + win[pl.ds(lead - 3 + tap, rows), :] * w_ref[tap:tap + 1, :]
        win[pl.ds(lead - 3, 3), :] = win[pl.ds(lead + rows - 3, 3), :]
        return jnp.where(valid, _silu(acc), 0.0)

    qc = conv(qwin, q_ref, wq_ref)
    kc = conv(kwin, k_ref, wk_ref)
    vc = conv(vwin, v_ref, wv_ref)
    bg = jnp.where(valid, bg_ref[0], 0.0)
    nh = bg.shape[1] // 2

    ri = lax.broadcasted_iota(jnp.int32, (rows, rows), 0)
    ci = lax.broadcasted_iota(jnp.int32, (rows, rows), 1)
    same = (ri // chunk) == (ci // chunk)
    tril = same & (ri >= ci)
    strict = same & (ri > ci)
    gc_all = _exact_left_dot(tril, bg)

    nv = 2 * g_heads
    qs = [_l2norm(qc[:, gi * dk:(gi + 1) * dk]) * (dk ** -0.5) for gi in range(g_heads)]
    ks = [_l2norm(kc[:, gi * dk:(gi + 1) * dk]) for gi in range(g_heads)]
    kks = [_bdot_nt(x, x) for x in ks]
    qks = [_bdot_nt(x, y) for x, y in zip(qs, ks)]
    per_v = lambda xs: [xs[sv // 2] for sv in range(nv)]
    q, k = per_v(qs), per_v(ks)
    v = [vc[:, sv * dk:(sv + 1) * dk] for sv in range(nv)]
    beta = [_col(bg, 2 * h * g_heads + sv) for sv in range(nv)]
    gc = [_col(gc_all, nh + 2 * h * g_heads + sv) for sv in range(nv)]
    uw, qg, qkd = _delta_wy(per_v(kks), per_v(qks), k, q, v, beta, gc, tril, strict, n_sq)
    ow = [_bdot(x, y) for x, y in zip(qkd, uw)]
    q_eff = [qg[i] - ow[i][:, dk:] for i in range(nv)]
    s = [s_ref[sv] for sv in range(nv)]
    o_parts = [[] for _ in range(nv)]
    for j in range(nb):
        sl = slice(j * chunk, (j + 1) * chunk)
        g_end = [g[(j + 1) * chunk - 1:(j + 1) * chunk, :] for g in gc]
        kd = [k[i][sl] * jnp.exp(g_end[i] - gc[i][sl]) for i in range(nv)]
        bn = [_bdot_tn(kd[i], uw[i][sl]) for i in range(nv)]
        qs_j = [_bdot(q_eff[i][sl], s[i]) for i in range(nv)]
        ns_j = [_bdot(bn[i][:, dk:], s[i]) for i in range(nv)]
        for i in range(nv):
            o_parts[i].append(qs_j[i] + ow[i][sl, :dk])
            s[i] = s[i] * jnp.exp(g_end[i]) - ns_j[i] + bn[i][:, :dk]
    outs = []
    for sv in range(nv):
        s_ref[sv] = s[sv]
        o = o_parts[sv][0] if nb == 1 else jnp.concatenate(o_parts[sv], axis=0)
        z = z_ref[0, :, sv * dk:(sv + 1) * dk]
        outs.append(o * _rms_scale(o) * ng_ref[...] * _silu(z))
    o_ref[0] = jnp.concatenate(outs, axis=1).astype(o_ref.dtype)

    @pl.when(c == nc - 1)
    def _():
        s_out_ref[0] = s_ref[...]


def gdn_prompt(raw, bg, conv_w, norm_g, rows, chunk, g_heads):
    b, t, _ = raw.shape
    nh = bg.shape[-1] // 2
    dk = GDN_HEAD
    n_qk = nh // 2
    qk_dim = n_qk * dk
    gq, gv = g_heads * dk, 2 * g_heads * dk
    n_hblk = n_qk // g_heads
    nc = pl.cdiv(t, rows)
    n_sq = int(math.log2(chunk)) - 1
    assert 2 ** (n_sq + 1) == chunk and n_qk % g_heads == 0 and rows % chunk == 0
    blk = lambda width, off: pl.BlockSpec((1, rows, width), lambda bi, h, c: (bi, c, off + h))
    wblk = lambda width, off: pl.BlockSpec((GDN_CONV_TAPS, width), lambda bi, h, c: (0, off + h))
    kernel = functools.partial(_gdn_prompt_kernel, seq_len=t, rows=rows, chunk=chunk, g_heads=g_heads,
                               n_sq=n_sq)
    return pl.pallas_call(
        kernel,
        grid=(b, n_hblk, nc),
        in_specs=[
            blk(gq, 0), blk(gq, n_hblk), blk(gv, n_hblk), blk(gv, 2 * n_hblk),
            pl.BlockSpec((1, rows, 2 * nh), lambda bi, h, c: (bi, c, 0)),
            wblk(gq, 0), wblk(gq, n_hblk), wblk(gv, n_hblk),
            pl.BlockSpec((1, dk), lambda bi, h, c: (0, 0)),
        ],
        out_specs=[
            pl.BlockSpec((1, rows, gv), lambda bi, h, c: (bi, c, h)),
            pl.BlockSpec((1, 2 * g_heads, dk, dk), lambda bi, h, c: (bi, h, 0, 0)),
        ],
        out_shape=[jax.ShapeDtypeStruct((b, t, 2 * qk_dim), BF16),
                   jax.ShapeDtypeStruct((b, nh, dk, dk), F32)],
        scratch_shapes=[pltpu.VMEM((8 + rows, gq), F32), pltpu.VMEM((8 + rows, gq), F32),
                        pltpu.VMEM((8 + rows, gv), F32), pltpu.VMEM((2 * g_heads, dk, dk), F32)],
        compiler_params=_params("parallel", "parallel", "arbitrary"),
        name="gdn_prompt",
    )(raw, raw, raw, raw, bg, conv_w, conv_w, conv_w, norm_g.reshape(1, dk))


def _gdn_sample_kernel(q_ref, k_ref, v_ref, z_ref, bg_ref, cq_ref, ck_ref, cv_ref, wq_ref, wk_ref, wv_ref,
                       ng_ref, s_in_ref, o_ref, s_out_ref, qwin, kwin, vwin, *, n_sq):
    h = pl.program_id(0)
    tb, t, dk = q_ref.shape
    r = tb * t
    lead = 8

    def conv(win, x_ref, c_ref, w_ref):
        win[:, pl.ds(lead - 3, 3), :] = c_ref[...]
        win[:, pl.ds(lead, t), :] = x_ref[...]
        acc = win[:, pl.ds(lead - 3, t), :] * w_ref[0:1, :]
        for i in range(1, 4):
            acc = acc + win[:, pl.ds(lead - 3 + i, t), :] * w_ref[i:i + 1, :]
        return _silu(acc).reshape(r, x_ref.shape[-1])

    qc = conv(qwin, q_ref, cq_ref, wq_ref)
    kc = conv(kwin, k_ref, ck_ref, wk_ref)
    vc = conv(vwin, v_ref, cv_ref, wv_ref)
    bg = bg_ref[...].reshape(r, bg_ref.shape[-1])
    nh = bg.shape[1] // 2

    ri = lax.broadcasted_iota(jnp.int32, (r, r), 0)
    ci = lax.broadcasted_iota(jnp.int32, (r, r), 1)
    same = (ri // t) == (ci // t)
    tril = same & (ri >= ci)
    strict = same & (ri > ci)
    gc_all = _exact_left_dot(tril, bg)
    g_sum = jnp.sum(bg_ref[...], axis=1, keepdims=True)

    q = _l2norm(qc) * (dk ** -0.5)
    k = _l2norm(kc)
    kk = _bdot_nt(k, k)
    qk = _bdot_nt(q, k)
    bdims = (((2,), (1,)), ((0,), (0,)))
    two = range(2)
    v = [vc[:, e * dk:(e + 1) * dk] for e in two]
    beta = [_col(bg, 2 * h + e) for e in two]
    gc = [_col(gc_all, nh + 2 * h + e) for e in two]
    uw, qg, qkd = _delta_wy([kk, kk], [qk, qk], [k, k], [q, q], v, beta, gc, tril, strict, n_sq)
    s = [s_in_ref[:, e] for e in two]
    wq3 = [jnp.concatenate([uw[e][:, dk:].reshape(tb, t, dk), qg[e].reshape(tb, t, dk)], axis=1).astype(BF16)
           for e in two]
    ws_qs = [lax.dot_general(wq3[e], s[e].astype(BF16), bdims, preferred_element_type=F32) for e in two]
    v_new = [uw[e][:, :dk] - ws_qs[e][:, :t].reshape(r, dk) for e in two]
    qkv = [_bdot(qkd[e], v_new[e]) for e in two]
    lane3 = lax.broadcasted_iota(jnp.int32, g_sum.shape, 2)
    g_last = [jnp.sum(jnp.where(lane3 == nh + 2 * h + e, g_sum, 0.0), axis=-1, keepdims=True) for e in two]
    kd = [(k.reshape(tb, t, dk) * jnp.exp(g_last[e] - gc[e].reshape(tb, t, 1))).astype(BF16) for e in two]
    upd = [lax.dot_general(kd[e], v_new[e].reshape(tb, t, dk).astype(BF16), (((1,), (1,)), ((0,), (0,))),
                           preferred_element_type=F32) for e in two]
    for e in two:
        s_out_ref[:, e] = s[e] * jnp.exp(g_last[e]) + upd[e]
        o = ws_qs[e][:, t:].reshape(r, dk) + qkv[e]
        z = z_ref[:, :, e * dk:(e + 1) * dk].reshape(r, dk)
        o_ref[:, :, e * dk:(e + 1) * dk] = (o * _rms_scale(o) * ng_ref[...] * _silu(z)).reshape(tb, t, dk)


def gdn_sample(raw, bg, conv_state, ssm_state, conv_w, norm_g, tb):
    b, t, _ = raw.shape
    nh = bg.shape[-1] // 2
    dk = GDN_HEAD
    n_qk = nh // 2
    n_sq = int(math.log2(t)) - 1
    assert 2 ** (n_sq + 1) == t and b % tb == 0 and t == 8
    blk = lambda width, off: pl.BlockSpec((tb, t, width), lambda h, i: (i, 0, off + h))
    cblk = lambda width, off: pl.BlockSpec((tb, 3, width), lambda h, i: (i, 0, off + h))
    wblk = lambda width, off: pl.BlockSpec((4, width), lambda h, i: (0, off + h))
    sblk = pl.BlockSpec((tb, 2, dk, dk), lambda h, i: (i, h, 0, 0))
    return pl.pallas_call(
        functools.partial(_gdn_sample_kernel, n_sq=n_sq),
        grid=(n_qk, b // tb),
        in_specs=[
            blk(dk, 0), blk(dk, n_qk), blk(2 * dk, n_qk), blk(2 * dk, 2 * n_qk),
            pl.BlockSpec((tb, t, 2 * nh), lambda h, i: (i, 0, 0)),
            cblk(dk, 0), cblk(dk, n_qk), cblk(2 * dk, n_qk),
            wblk(dk, 0), wblk(dk, n_qk), wblk(2 * dk, n_qk),
            pl.BlockSpec((1, dk), lambda h, i: (0, 0)),
            sblk,
        ],
        out_specs=[pl.BlockSpec((tb, t, 2 * dk), lambda h, i: (i, 0, h)), sblk],
        out_shape=[jax.ShapeDtypeStruct((b, t, nh * dk), F32),
                   jax.ShapeDtypeStruct(ssm_state.shape, F32)],
        scratch_shapes=[pltpu.VMEM((tb, 8 + t, dk), F32), pltpu.VMEM((tb, 8 + t, dk), F32),
                        pltpu.VMEM((tb, 8 + t, 2 * dk), F32)],
        compiler_params=_params("parallel", "parallel"),
        name="gdn_sample",
    )(raw, raw, raw, raw, bg, conv_state, conv_state, conv_state, conv_w, conv_w, conv_w,
      norm_g.reshape(1, dk), ssm_state)


def _t5_bucket(dist):
    max_exact = NUM_BUCKETS // 2
    n = jnp.maximum(dist, 0)
    log_ratio = jnp.log(jnp.maximum(n, 1).astype(F32) * (1.0 / max_exact)) * (1.0 / math.log(MAX_DISTANCE / max_exact))
    large = jnp.minimum(max_exact + (log_ratio * (NUM_BUCKETS - max_exact)).astype(jnp.int32), NUM_BUCKETS - 1)
    return jnp.where(n < max_exact, n, large)


def _prompt_bias_kernel(rb_ref, o_ref, *, tq, tk, n_meta):
    ty = pl.program_id(0)
    m = pl.program_id(1)
    ii = lax.broadcasted_iota(jnp.int32, (tq, tk), 0)
    jj = lax.broadcasted_iota(jnp.int32, (tq, tk), 1)
    dist = n_meta + (ty - 1) * tk + ii - jj
    bucket = _t5_bucket(dist)
    tile = jnp.full((tq, tk), NEG_BIG, F32)
    for bkt in range(NUM_BUCKETS):
        tile = jnp.where((bucket == bkt) & (dist >= 0), rb_ref[bkt, m], tile)
    o_ref[0, 0] = tile


def prompt_bias_tiles(rel_bias, tq, tk, n_meta):
    n_maps = rel_bias.shape[1]
    return pl.pallas_call(
        functools.partial(_prompt_bias_kernel, tq=tq, tk=tk, n_meta=n_meta),
        grid=(3, n_maps),
        in_specs=[pl.BlockSpec(memory_space=pltpu.SMEM)],
        out_specs=pl.BlockSpec((1, 1, tq, tk), lambda ty, m: (ty, m, 0, 0)),
        out_shape=jax.ShapeDtypeStruct((3, n_maps, tq, tk), F32),
        compiler_params=_params("parallel", "parallel"),
        name="prompt_bias",
    )(rel_bias.astype(F32))


def _lambda_of(lv_ref, lam_init):
    lv = lv_ref[...].astype(F32)
    a = jnp.sum(lv[0:1] * lv[1:2], axis=-1, keepdims=True)
    b = jnp.sum(lv[2:3] * lv[3:4], axis=-1, keepdims=True)
    return jnp.exp(a) - jnp.exp(b) + lam_init


def _attn_prompt_kernel(qi_ref, kj_ref, q_ref, k_ref, v_ref, bias_ref, rb_ref, lv_ref, sub_ref, o_ref,
                        m_ref, l_ref, acc_ref, *, n_keys, nk, tk, hp, lam_init, scale):
    h = pl.program_id(1)
    step = pl.program_id(2)
    qi = qi_ref[step]
    kj = kj_ref[step]
    dh = DIFF_HEAD
    hw = 2 * dh

    @pl.when(kj == 0)
    def _():
        m_ref[...] = jnp.full_like(m_ref, NEG_BIG)
        l_ref[...] = jnp.zeros_like(l_ref)
        acc_ref[...] = jnp.zeros_like(acc_ref)

    krow = lax.broadcasted_iota(jnp.int32, (tk, 1), 0)
    kvalid = (kj * tk + krow) < n_keys
    k = jnp.where(kvalid, k_ref[0], 0.0).astype(BF16)
    v = jnp.where(kvalid, v_ref[0], 0.0).astype(BF16)
    near = (qi - kj) <= 1
    lanes = m_ref.shape[-1]
    wide = lambda x, n: jnp.concatenate([x] * (n // lanes), axis=1)
    ones = jnp.ones((tk, lanes), BF16)
    maps = range(2 * hp)
    s = [_bdot_nt((q_ref[0, :, mi * dh:(mi + 1) * dh] * scale).astype(BF16), k[:, mi * dh:(mi + 1) * dh])
         for mi in maps]
    ps, corrs = [], []
    for mi in maps:
        far_bias = rb_ref[NUM_BUCKETS - 1, 2 * hp * h + mi]
        sb = s[mi] + jnp.where(near, bias_ref[0, mi], far_bias)
        m_old = m_ref[mi]
        m_new = jnp.maximum(m_old, jnp.max(sb, axis=-1, keepdims=True))
        m_ref[mi] = m_new
        corrs.append(jnp.exp(m_old - m_new))
        ps.append(jnp.exp(sb - wide(m_new, tk)).astype(BF16))
    sums = [jnp.dot(ps[mi], ones, preferred_element_type=F32) for mi in maps]
    pvs = [jnp.dot(ps[mi], v[:, (mi // 2) * hw:(mi // 2 + 1) * hw], preferred_element_type=F32) for mi in maps]
    for mi in maps:
        l_ref[mi] = l_ref[mi] * corrs[mi] + sums[mi]
        acc_ref[mi] = acc_ref[mi] * wide(corrs[mi], hw) + pvs[mi]

    @pl.when(kj == jnp.minimum(qi + 1, nk - 1))
    def _():
        lam = _lambda_of(lv_ref, lam_init)
        for hh in range(hp):
            o = (acc_ref[2 * hh] / wide(l_ref[2 * hh], hw)
                 - lam * (acc_ref[2 * hh + 1] / wide(l_ref[2 * hh + 1], hw)))
            o_ref[0, :, hh * hw:(hh + 1) * hw] = (
                o * _rms_scale(o) * sub_ref[...] * (1.0 - lam_init)).astype(o_ref.dtype)


def attn_prompt(q, k, v, bias_tiles, rel_bias, lam_vecs, subln, lam_init, tq, tk, hp):
    b, s, d = q.shape
    n_keys = k.shape[1]
    n_heads = d // (2 * DIFF_HEAD)
    hw = 2 * DIFF_HEAD
    bw = hp * hw
    nq, nk = s // tq, pl.cdiv(n_keys, tk)
    assert s % tq == 0 and tq == tk and N_META <= tk and n_heads % hp == 0
    assert N_META + 2 * tk - (tk - 1) >= MAX_DISTANCE
    pairs = [(qi, kj) for qi in range(nq) for kj in range(min(qi + 1, nk - 1) + 1)]
    qi_tab = jnp.asarray([p[0] for p in pairs], jnp.int32)
    kj_tab = jnp.asarray([p[1] for p in pairs], jnp.int32)

    qmap = lambda bi, h, st, qt, kt: (bi, qt[st], h)
    kvmap = lambda bi, h, st, qt, kt: (bi, kt[st], h)
    bias_map = lambda bi, h, st, qt, kt: (jnp.clip(qt[st] - kt[st] + 1, 0, 2), h, 0, 0)
    const2 = lambda bi, h, st, qt, kt: (0, 0)
    kernel = functools.partial(_attn_prompt_kernel, n_keys=n_keys, nk=nk, tk=tk, hp=hp, lam_init=lam_init,
                               scale=DIFF_HEAD ** -0.5)
    grid_spec = pltpu.PrefetchScalarGridSpec(
        num_scalar_prefetch=2,
        grid=(b, n_heads // hp, len(pairs)),
        in_specs=[
            pl.BlockSpec((1, tq, bw), qmap),
            pl.BlockSpec((1, tk, bw), kvmap),
            pl.BlockSpec((1, tk, bw), kvmap),
            pl.BlockSpec((1, 2 * hp, tq, tk), bias_map),
            pl.BlockSpec(memory_space=pltpu.SMEM),
            pl.BlockSpec(lam_vecs.shape, const2),
            pl.BlockSpec((1, hw), const2),
        ],
        out_specs=pl.BlockSpec((1, tq, bw), qmap),
        scratch_shapes=[pltpu.VMEM((2 * hp, tq, LANES), F32), pltpu.VMEM((2 * hp, tq, LANES), F32),
                        pltpu.VMEM((2 * hp, tq, hw), F32)],
    )
    return pl.pallas_call(
        kernel,
        grid_spec=grid_spec,
        out_shape=jax.ShapeDtypeStruct((b, s, d), BF16),
        compiler_params=_params("parallel", "parallel", "arbitrary"),
        name="attn_prompt",
    )(qi_tab, kj_tab, q, k, v, bias_tiles, rel_bias.astype(F32), lam_vecs, subln.reshape(1, hw))


def _sample_bias_kernel(rb_ref, o_ref, *, t, page, past_len):
    r = o_ref.shape[1]
    ri = lax.broadcasted_iota(jnp.int32, (r, page), 0)
    jj = lax.broadcasted_iota(jnp.int32, (r, page), 1)
    tq = ri % t
    far = rb_ref[:, NUM_BUCKETS - 1:NUM_BUCKETS]
    o_ref[0] = jnp.broadcast_to(far, (r, page))
    for ty, dist, ok in ((1, page + tq - jj, None), (2, tq - jj, (jj <= tq) & (jj < t))):
        bucket = _t5_bucket(dist)
        tile = jnp.full((r, page), NEG_BIG, F32)
        for bkt in range(NUM_BUCKETS):
            hit = (bucket == bkt) if ok is None else ((bucket == bkt) & ok)
            tile = jnp.where(hit, rb_ref[:, bkt:bkt + 1], tile)
        o_ref[ty] = tile
    del past_len


def sample_bias_tiles(rel_bias, t, page, past_len):
    n_maps = rel_bias.shape[1]
    assert past_len % page == 0 and past_len - page >= MAX_DISTANCE
    rb_rows = jnp.repeat(rel_bias.astype(F32).T, t, axis=0)
    return pl.pallas_call(
        functools.partial(_sample_bias_kernel, t=t, page=page, past_len=past_len),
        out_shape=jax.ShapeDtypeStruct((3, n_maps * t, page), F32),
        compiler_params=pltpu.CompilerParams(vmem_limit_bytes=VMEM_LIMIT),
        name="sample_bias",
    )(rb_rows)


def _attn_sample_kernel(*refs, lam_init, scale, page, pps):
    pt_ref, q_ref = refs[0], refs[1]
    kc_refs, vlo_refs, vhi_refs = refs[2:2 + pps], refs[2 + pps:2 + 2 * pps], refs[2 + 2 * pps:2 + 3 * pps]
    (kn_ref, vn_ref, far_ref, bias_ref, nbias_ref, lv_ref, sub_ref, o_ref, m_ref, l_ref, acc_ref) = refs[2 + 3 * pps:]
    del pt_ref
    p = pl.program_id(1)
    n_steps = pl.num_programs(1)
    t, d = q_ref.shape[1], q_ref.shape[2]
    dh = DIFF_HEAD
    hw = 2 * dh
    n_maps = d // dh
    n_heads = n_maps // 2

    @pl.when(p == 0)
    def _():
        m_ref[...] = jnp.full_like(m_ref, NEG_BIG)
        l_ref[...] = jnp.zeros_like(l_ref)
        acc_ref[...] = jnp.zeros_like(acc_ref)

    def update(k_ofs, v_ofs, biases):
        s = jnp.concatenate(
            [jnp.concatenate([_bdot_nt(q_ref[0, :, m * dh:(m + 1) * dh] * scale, k_of(m)) for m in range(n_maps)],
                             axis=0) + bias for k_of, bias in zip(k_ofs, biases)], axis=1)
        m_old = m_ref[...]
        m_new = jnp.maximum(m_old, jnp.max(s, axis=-1, keepdims=True))
        corr = jnp.exp(m_old - m_new)
        pr = jnp.exp(s - m_new)
        l_ref[...] = l_ref[...] * corr + jnp.sum(pr, axis=-1, keepdims=True)
        pv = jnp.concatenate(
            [_bdot(pr[2 * h * t:(2 * h + 2) * t], jnp.concatenate([v_of(h) for v_of in v_ofs], axis=0))
             for h in range(n_heads)], axis=0)
        acc_ref[...] = acc_ref[...] * corr + pv
        m_ref[...] = m_new

    strided_k = lambda ref: (lambda m: ref[pl.ds(m, page, stride=n_maps), :])
    strided_v = lambda lo, hi: (lambda h: jnp.concatenate([lo[pl.ds(h, page, stride=n_heads), :],
                                                           hi[pl.ds(h, page, stride=n_heads), :]], axis=1))
    update([strided_k(r) for r in kc_refs], [strided_v(lo, hi) for lo, hi in zip(vlo_refs, vhi_refs)],
           [far_ref[0]] * (pps - 1) + [bias_ref[0]])

    @pl.when(p == n_steps - 1)
    def _():
        kn, vn = kn_ref[0], vn_ref[0]
        zk = jnp.zeros((page - t, dh), F32)
        zv = jnp.zeros((page - t, hw), F32)
        update([lambda m: jnp.concatenate([kn[:, m * dh:(m + 1) * dh], zk], axis=0)],
               [lambda h: jnp.concatenate([vn[:, h * hw:(h + 1) * hw], zv], axis=0)], [nbias_ref[0]])
        lam = _lambda_of(lv_ref, lam_init)
        for h in range(n_heads):
            r0 = 2 * h * t
            o0 = acc_ref[pl.ds(r0, t), :] / l_ref[pl.ds(r0, t), :]
            o1 = acc_ref[pl.ds(r0 + t, t), :] / l_ref[pl.ds(r0 + t, t), :]
            o = o0 - lam * o1
            o_ref[0, :, h * hw:(h + 1) * hw] = o * _rms_scale(o) * sub_ref[...] * (1.0 - lam_init)


def attn_sample(q, k_new, v_new, cache_k, cache_v, page_table, bias_tiles, lam_vecs, subln, lam_init, pps):
    b, t, d = q.shape
    n_pages = page_table.shape[1]
    dh = DIFF_HEAD
    hw = 2 * dh
    n_maps = d // dh
    page = cache_k.shape[1] // n_maps
    r = n_maps * t
    n_steps = n_pages // pps
    assert r == bias_tiles.shape[1] and page == bias_tiles.shape[2] and cache_v.shape[1] == page * n_maps // 2
    assert n_pages % pps == 0

    def cache_map(i, half):
        return lambda bi, p, pt: (pt[bi, p * pps + i], 0, half)

    row = lambda bi, p, pt: (bi, 0, 0)
    tile = lambda index: pl.BlockSpec((1, r, page), index)
    grid_spec = pltpu.PrefetchScalarGridSpec(
        num_scalar_prefetch=1,
        grid=(b, n_steps),
        in_specs=[pl.BlockSpec((1, t, d), row)]
        + [pl.BlockSpec((None, page * n_maps, dh), cache_map(i, 0)) for i in range(pps)]
        + [pl.BlockSpec((None, page * n_maps // 2, dh), cache_map(i, 0)) for i in range(pps)]
        + [pl.BlockSpec((None, page * n_maps // 2, dh), cache_map(i, 1)) for i in range(pps)]
        + [
            pl.BlockSpec((1, t, d), row),
            pl.BlockSpec((1, t, d), row),
            tile(lambda bi, p, pt: (0, 0, 0)),
            tile(lambda bi, p, pt: (jnp.where(p == n_steps - 1, 1, 0), 0, 0)),
            tile(lambda bi, p, pt: (2, 0, 0)),
            pl.BlockSpec(lam_vecs.shape, lambda bi, p, pt: (0, 0)),
            pl.BlockSpec((1, hw), lambda bi, p, pt: (0, 0)),
        ],
        out_specs=pl.BlockSpec((1, t, d), row),
        scratch_shapes=[pltpu.VMEM((r, 1), F32), pltpu.VMEM((r, 1), F32), pltpu.VMEM((r, hw), F32)],
    )
    return pl.pallas_call(
        functools.partial(_attn_sample_kernel, lam_init=lam_init, scale=dh ** -0.5, page=page, pps=pps),
        grid_spec=grid_spec,
        out_shape=jax.ShapeDtypeStruct((b, t, d), F32),
        compiler_params=_params("parallel", "arbitrary"),
        name="attn_sample",
    )(page_table, q, *([cache_k] * pps), *([cache_v] * (2 * pps)), k_new, v_new, bias_tiles, bias_tiles,
      bias_tiles, lam_vecs, subln.reshape(1, hw))


TN = 512
TN_MM = 1024
ATTN_TILE = 256
ATTN_HEADS_PER_STEP = 8
GDN_CHUNK = 64
GDN_ROWS = 128
GDN_HEADS_PER_STEP = 8
GDN_SEQS_PER_STEP = 16
PAGES_PER_STEP = 8


def _row_tile(m, target):
    best = 16
    for t in range(16, target + 1, 16):
        if m % t == 0:
            best = t
    assert m % best == 0
    return best


def _trunk(h, n_seq, n_drop, gdn_fn, attn_fn, kv_fn, ln_gain, ffn_w_in, ffn_w_out, gdn_w_out, kv_norm,
           diff_w_q, diff_w_out, final_norm, n_a, tm_target):
    depth = ln_gain.shape[0]
    d = h.shape[-1]
    tm = _row_tile(h.shape[0], tm_target)
    hn = rmsnorm(h, ln_gain[0, 0], tm)
    conv_states, ssm_states, kv = [], [], None
    y = None
    row0 = 0
    for i in range(depth):
        rows = h.shape[0] if h.ndim == 2 else h.shape[0] * (h.shape[1] - row0)
        tm = _row_tile(rows, tm_target)
        act = ffn_in(hn, ffn_w_in, i, 0, tm, TN, row0)
        h, (hn,) = mm_res_norm(act, ffn_w_out, (i, 0), h, ln_gain[i, 1][None], [BF16], 0.5, tm_target, row0)
        row0 = 0
        if i < n_a:
            mix, conv_new, ssm_new = gdn_fn(i, hn, tm)
            conv_states.append(conv_new)
            ssm_states.append(ssm_new)
            h, (hn,) = mm_res_norm(mix, gdn_w_out, (i,), h, ln_gain[i, 2][None], [BF16], 1.0, tm_target)
        else:
            j = i - n_a
            q = mm(hn, diff_w_q, (j,), 0, d, tm, TN_MM)
            att = attn_fn(j, i, q, kv)
            h, (hn,) = mm_res_norm(att, diff_w_out, (j,), h, ln_gain[i, 2][None], [BF16], 1.0, tm_target)
        act = ffn_in(hn, ffn_w_in, i, 1, tm, TN)
        if i == depth - 1:
            _, (y,) = mm_res_norm(act, ffn_w_out, (i, 1), h, final_norm[None], [F32], 0.5, tm_target)
        elif i == n_a - 1:
            gains = jnp.stack([kv_norm, ln_gain[i + 1, 0]])
            h, (hkv, hn) = mm_res_norm(act, ffn_w_out, (i, 1), h, gains, [BF16, BF16], 0.5, tm_target)
            kv = kv_fn(hkv, tm)
            if n_drop:
                t = h.shape[0] // n_seq
                assert (t - n_drop) % 16 == 0
                h, hn, row0 = h.reshape(n_seq, t, d), hn.reshape(n_seq, t, d), n_drop
        else:
            h, (hn,) = mm_res_norm(act, ffn_w_out, (i, 1), h, ln_gain[i + 1, 0][None], [BF16], 0.5, tm_target)
    return y, kv, jnp.stack(conv_states), jnp.stack(ssm_states)


def kernel(x_prompt, x_sample, cache_k, cache_v, state_conv, state_ssm, page_table, meta_tokens, ln_gain,
           ffn_w_in, ffn_w_out, gdn_w_in, gdn_conv, gdn_a_log, gdn_dt_bias, gdn_norm, gdn_w_out, kv_norm,
           w_kv, diff_w_q, diff_lambda, diff_subln, diff_w_out, rel_bias, final_norm):
    b, seq, d = x_prompt.shape
    db, dseq, _ = x_sample.shape
    n_a = gdn_w_in.shape[0]
    n_heads_v = gdn_a_log.shape[1]
    qk_dim = (n_heads_v // 2) * GDN_HEAD
    conv_dim = gdn_conv.shape[-1]
    main_cols = conv_dim + n_heads_v * GDN_HEAD
    n_meta = meta_tokens.shape[0]
    t_prompt = n_meta + seq
    kv_cols = w_kv.shape[1] // 2
    n_maps = rel_bias.shape[1]
    page = cache_k.shape[1]
    past_len = page_table.shape[1] * page
    assert n_meta == N_META and conv_dim == 2 * qk_dim + n_heads_v * GDN_HEAD
    lam_init = lambda layer: 0.8 - 0.6 * math.exp(-0.3 * layer)

    def gdn_proj(i, hn, tm):
        raw = mm(hn, gdn_w_in, (i,), 0, main_cols, tm, TN_MM)
        bg = gdn_ba(hn, gdn_w_in[i][:, main_cols:], gdn_a_log[i], gdn_dt_bias[i], tm)
        return raw, bg

    def kv_proj(hkv, tm):
        return (mm(hkv, w_kv, (), 0, kv_cols, tm, TN_MM), mm(hkv, w_kv, (), kv_cols, kv_cols, tm, TN_MM))

    def gdn_prompt_fn(i, hn, tm):
        raw, bg = gdn_proj(i, hn, tm)
        raw3 = raw.reshape(b, t_prompt, main_cols)
        o, ssm = gdn_prompt(raw3, bg.reshape(b, t_prompt, -1), gdn_conv[i], gdn_norm[i], GDN_ROWS, GDN_CHUNK,
                            GDN_HEADS_PER_STEP)
        return o.reshape(b * t_prompt, -1), raw3[:, t_prompt - GDN_CONV_TAPS + 1:, :conv_dim], ssm

    prompt_bias = prompt_bias_tiles(rel_bias, ATTN_TILE, ATTN_TILE, n_meta)

    def attn_prompt_fn(j, layer, q, kv):
        k, v = kv
        o = attn_prompt(q.reshape(b, seq, d), k.reshape(b, t_prompt, kv_cols), v.reshape(b, t_prompt, kv_cols),
                        prompt_bias, rel_bias, diff_lambda[j], diff_subln[j], lam_init(layer), ATTN_TILE, ATTN_TILE,
                        ATTN_HEADS_PER_STEP)
        return o.reshape(b * seq, d)

    meta = jnp.broadcast_to(meta_tokens.astype(x_prompt.dtype)[None], (b, n_meta, d))
    h_prompt = jnp.concatenate([meta, x_prompt], axis=1).reshape(b * t_prompt, d)
    shared = (ln_gain, ffn_w_in, ffn_w_out.astype(BF16), gdn_w_out.astype(BF16), kv_norm, diff_w_q,
              diff_w_out.astype(BF16), final_norm, n_a)
    y_p, (k_p, v_p), conv_p, ssm_p = _trunk(h_prompt, b, n_meta, gdn_prompt_fn, attn_prompt_fn, kv_proj,
                                            *shared, tm_target=768)

    def gdn_sample_fn(i, hn, tm):
        raw, bg = gdn_proj(i, hn, tm)
        raw3 = raw.reshape(db, dseq, main_cols)
        o, ssm = gdn_sample(raw3, bg.reshape(db, dseq, -1), state_conv[i], state_ssm[i], gdn_conv[i],
                            gdn_norm[i], GDN_SEQS_PER_STEP)
        return o.reshape(db * dseq, -1), raw3[:, dseq - 3:, :conv_dim], ssm

    sample_bias = sample_bias_tiles(rel_bias, dseq, page, past_len)
    cache_k2 = cache_k.reshape(cache_k.shape[0], page * cache_k.shape[2], cache_k.shape[3])
    cache_v2 = cache_v.reshape(cache_v.shape[0], page * cache_v.shape[2], cache_v.shape[3])

    def attn_sample_fn(j, layer, q, kv):
        k, v = kv
        r3 = lambda x: x.reshape(db, dseq, kv_cols)
        o = attn_sample(r3(q), r3(k), r3(v), cache_k2, cache_v2, page_table, sample_bias, diff_lambda[j],
                        diff_subln[j], lam_init(layer), PAGES_PER_STEP)
        return o.reshape(db * dseq, d)

    y_s, (k_s, v_s), conv_s, ssm_s = _trunk(x_sample.reshape(db * dseq, d), db, 0, gdn_sample_fn,
                                            attn_sample_fn, kv_proj, *shared, tm_target=1024)

    hd = DIFF_HEAD
    return (y_p.reshape(b, seq, d), y_s.reshape(db, dseq, d),
            k_p.reshape(b, t_prompt, n_maps, hd), v_p.reshape(b, t_prompt, n_maps // 2, 2 * hd),
            conv_p, ssm_p,
            k_s.reshape(db, dseq, n_maps, hd), v_s.reshape(db, dseq, n_maps // 2, 2 * hd),
            conv_s, ssm_s)
```

```python
import functools
import math

import jax
import jax.numpy as jnp
from jax import lax
from jax.experimental import pallas as pl
from jax.experimental.pallas import tpu as pltpu

F32 = jnp.float32
BF16 = jnp.bfloat16

EPS = 1e-6
L2_EPS = 1e-6
N_META = 16
GDN_HEAD = 128
GDN_CONV_TAPS = 4
DIFF_HEAD = 128
NUM_BUCKETS = 32
MAX_DISTANCE = 128
NEG_BIG = -1e30
LANES = 128
VMEM_LIMIT = 56 * 1024 * 1024
VMEM_BUDGET = 44 * 1024 * 1024
RES_TM_MAX = 768


def _params(*sem):
    return pltpu.CompilerParams(dimension_semantics=sem, vmem_limit_bytes=VMEM_LIMIT)


def _bdot(a, b):
    return jnp.dot(a.astype(BF16), b.astype(BF16), preferred_element_type=F32)


def _bdot_nt(a, b):
    return lax.dot_general(a.astype(BF16), b.astype(BF16), (((1,), (1,)), ((), ())),
                           preferred_element_type=F32)


def _bdot_tn(a, b):
    return lax.dot_general(a.astype(BF16), b.astype(BF16), (((0,), (0,)), ((), ())),
                           preferred_element_type=F32)


def _hdot(a, b):
    return jnp.dot(a, b, precision=lax.Precision.HIGHEST, preferred_element_type=F32)


def _split3(x):
    hi = x.astype(BF16)
    r1 = x - hi.astype(F32)
    mid = r1.astype(BF16)
    lo = (r1 - mid.astype(F32)).astype(BF16)
    return hi, mid, lo


def _exact_left_dot(mask01, x):
    m = mask01.astype(BF16)
    hi, mid, lo = _split3(x)
    d = functools.partial(jnp.dot, preferred_element_type=F32)
    return d(m, hi) + d(m, mid) + d(m, lo)


def _silu(x):
    return x * jax.nn.sigmoid(x)


def _rms_scale(x):
    return lax.rsqrt(jnp.mean(x * x, axis=-1, keepdims=True) + EPS)


def _rmsnorm_kernel(x_ref, g_ref, o_ref):
    x = x_ref[...].astype(F32)
    o_ref[...] = (x * _rms_scale(x) * g_ref[...]).astype(o_ref.dtype)


def rmsnorm(x, gain, tm):
    m, d = x.shape
    return pl.pallas_call(
        _rmsnorm_kernel,
        grid=(pl.cdiv(m, tm),),
        in_specs=[pl.BlockSpec((tm, d), lambda i: (i, 0)), pl.BlockSpec((1, d), lambda i: (0, 0))],
        out_specs=pl.BlockSpec((tm, d), lambda i: (i, 0)),
        out_shape=jax.ShapeDtypeStruct((m, d), BF16),
        compiler_params=_params("parallel"),
        name="rmsnorm",
    )(x, gain.reshape(1, d))


def _row_tile_spec(x, tm, row0, tile_of, **kwargs):
    if x.ndim == 2:
        return pl.BlockSpec((tm, x.shape[1]), lambda *g: (tile_of(*g), 0), **kwargs), x.shape[0]
    n_seq, t_full, d = x.shape
    t = t_full - row0
    tiles_per_seq = t // tm
    assert t % tm == 0 and row0 % 16 == 0

    def index(*g):
        i = tile_of(*g)
        return (i // tiles_per_seq, pl.multiple_of(row0 + (i % tiles_per_seq) * tm, 16), 0)

    return pl.BlockSpec((pl.Element(1), pl.Element(tm), pl.Element(d)), index, **kwargs), n_seq * t


def _ffn_in_kernel(x_ref, wg_ref, wu_ref, o_ref):
    x = x_ref[...].reshape(x_ref.shape[-2:])
    g = jnp.dot(x, wg_ref[...].astype(BF16), preferred_element_type=F32)
    u = jnp.dot(x, wu_ref[...].astype(BF16), preferred_element_type=F32)
    o_ref[...] = (_silu(g) * u).astype(o_ref.dtype)


def ffn_in(x, w_in, layer, which, tm, tn, row0=0):
    d = x.shape[-1]
    d_ff = w_in.shape[-1] // 2
    ncol = d_ff // tn
    assert d_ff % tn == 0
    xspec, m = _row_tile_spec(x, tm, row0, lambda j, i: i)
    wspec = lambda off: pl.BlockSpec((None, None, d, tn), lambda j, i: (layer, which, 0, j + off))
    return pl.pallas_call(
        _ffn_in_kernel,
        grid=(ncol, pl.cdiv(m, tm)),
        in_specs=[xspec, wspec(0), wspec(ncol)],
        out_specs=pl.BlockSpec((tm, tn), lambda j, i: (i, j)),
        out_shape=jax.ShapeDtypeStruct((m, d_ff), BF16),
        compiler_params=_params("parallel", "parallel"),
        name="ffn_in",
    )(x, w_in, w_in)


def _mm_res_norm_kernel(a_ref, w_ref, h_ref, g_ref, oh_ref, *on_refs, scale, nk):
    k = pl.program_id(1)

    @pl.when(k == 0)
    def _():
        oh_ref[...] = h_ref[...].reshape(oh_ref.shape)

    oh_ref[...] += scale * _bdot(a_ref[...], w_ref[...])

    @pl.when(k == nk - 1)
    def _():
        h = oh_ref[...]
        hs = h * _rms_scale(h)
        for n, on in enumerate(on_refs):
            on[...] = (hs * g_ref[n:n + 1, :]).astype(on.dtype)


def _res_norm_tiles(m, kdim, d, a_dtype, w_dtype, out_dtypes, tm_target):
    tm = _row_tile(m, min(tm_target, RES_TM_MAX))
    size = lambda dt: jnp.dtype(dt).itemsize
    fixed = 3 * tm * d * 4 + tm * d * 4
    fixed += sum(2 * tm * d * size(dt) for dt in out_dtypes)
    tk = LANES
    for cand in range(LANES, kdim + 1, LANES):
        per_k = 2 * cand * (d * size(w_dtype) + tm * size(a_dtype))
        if kdim % cand == 0 and fixed + per_k <= VMEM_BUDGET:
            tk = cand
    return tm, tk


def mm_res_norm(a, w, w_idx, h, gains, out_dtypes, scale, tm_target, row0=0):
    m, kdim = a.shape
    d = h.shape[-1]
    tm, tk = _res_norm_tiles(m, kdim, d, a.dtype, w.dtype, out_dtypes, tm_target)
    nk = kdim // tk
    n_norm = gains.shape[0]
    lead = (None,) * len(w_idx)
    row = lambda i, k: (i, 0)
    hspec, h_rows = _row_tile_spec(h, tm, row0, lambda i, k: i, pipeline_mode=pl.Buffered(1))
    assert h_rows == m
    outs = pl.pallas_call(
        functools.partial(_mm_res_norm_kernel, scale=scale, nk=nk),
        grid=(pl.cdiv(m, tm), nk),
        in_specs=[
            pl.BlockSpec((tm, tk), lambda i, k: (i, k)),
            pl.BlockSpec(lead + (tk, d), lambda i, k: tuple(w_idx) + (k, 0)),
            hspec,
            pl.BlockSpec((n_norm, d), lambda i, k: (0, 0)),
        ],
        out_specs=[pl.BlockSpec((tm, d), row)] * (1 + n_norm),
        out_shape=[jax.ShapeDtypeStruct((m, d), F32)]
        + [jax.ShapeDtypeStruct((m, d), dt) for dt in out_dtypes],
        compiler_params=_params("parallel", "arbitrary"),
        name="mm_res_norm",
    )(a, w, h, gains)
    return outs[0], outs[1:]


def _mm_kernel(x_ref, w_ref, o_ref):
    o_ref[...] = _bdot(x_ref[...], w_ref[...]).astype(o_ref.dtype)


def mm(x, w, w_idx, col0, ncols, tm, tn):
    m, kdim = x.shape
    assert col0 % tn == 0 and ncols % tn == 0
    lead = (None,) * len(w_idx)
    c0 = col0 // tn
    return pl.pallas_call(
        _mm_kernel,
        grid=(ncols // tn, pl.cdiv(m, tm)),
        in_specs=[pl.BlockSpec((tm, kdim), lambda j, i: (i, 0)),
                  pl.BlockSpec(lead + (kdim, tn), lambda j, i: tuple(w_idx) + (0, c0 + j))],
        out_specs=pl.BlockSpec((tm, tn), lambda j, i: (i, j)),
        out_shape=jax.ShapeDtypeStruct((m, ncols), F32),
        compiler_params=_params("parallel", "parallel"),
        name="mm",
    )(x, w)


def _gdn_ba_kernel(x_ref, w_ref, alog_ref, dtb_ref, o_ref):
    r = _bdot(x_ref[...], w_ref[...])[:, :o_ref.shape[1]]
    nh = r.shape[1] // 2
    lane = lax.broadcasted_iota(jnp.int32, r.shape, 1)
    xa = r + dtb_ref[...]
    softplus = jnp.maximum(xa, 0.0) + jnp.log1p(jnp.exp(-jnp.abs(xa)))
    g = -jnp.exp(alog_ref[...]) * softplus
    o_ref[...] = jnp.where(lane < nh, jax.nn.sigmoid(r), g)


def gdn_ba(x, w, layer, col0, a_log, dt_bias, tm):
    m, d = x.shape
    nh = a_log.shape[0]
    assert col0 % LANES == 0 and w.shape[-1] - col0 == 2 * nh <= LANES
    pad = jnp.zeros((nh,), F32)
    alog = jnp.concatenate([pad, a_log.astype(F32)]).reshape(1, 2 * nh)
    dtb = jnp.concatenate([pad, dt_bias.astype(F32)]).reshape(1, 2 * nh)
    return pl.pallas_call(
        _gdn_ba_kernel,
        grid=(pl.cdiv(m, tm),),
        in_specs=[pl.BlockSpec((tm, d), lambda i: (i, 0)),
                  pl.BlockSpec((None, d, LANES), lambda i: (layer, 0, col0 // LANES)),
                  pl.BlockSpec((1, 2 * nh), lambda i: (0, 0)), pl.BlockSpec((1, 2 * nh), lambda i: (0, 0))],
        out_specs=pl.BlockSpec((tm, 2 * nh), lambda i: (i, 0)),
        out_shape=jax.ShapeDtypeStruct((m, 2 * nh), F32),
        compiler_params=_params("parallel"),
        name="gdn_ba",
    )(x, w, alog, dtb)


def _l2norm(x):
    return x * lax.rsqrt(jnp.sum(x * x, axis=-1, keepdims=True) + L2_EPS)


def _col(x, idx):
    lane = lax.broadcasted_iota(jnp.int32, x.shape, 1)
    return jnp.sum(jnp.where(lane == idx, x, 0.0), axis=-1, keepdims=True)


def _row_of(col):
    r = col.shape[0]
    return jnp.broadcast_to(col, (r, max(r, 128))).T[:r, :]


def _unit_lower_inverse_minus_eye(a_list, n_sq):
    p = [-a for a in a_list]
    t = list(p)
    for _ in range(n_sq):
        p = [_bdot(x, x) for x in p]
        tp = [_bdot(x, y) for x, y in zip(t, p)]
        t = [x + y + z for x, y, z in zip(t, p, tp)]
    return t


def _delta_wy(kk, qk, k, q, v, beta, gc, tril, strict, n_sq):
    n = len(v)
    decay = [jnp.exp(jnp.where(tril, gc[i] - _row_of(gc[i]), -jnp.inf)) for i in range(n)]
    a = [jnp.where(strict, beta[i] * kk[i] * decay[i], 0.0) for i in range(n)]
    t = _unit_lower_inverse_minus_eye(a, n_sq)
    egc = [jnp.exp(g) for g in gc]
    rhs = [jnp.concatenate([v[i] * beta[i], k[i] * (beta[i] * egc[i])], axis=1) for i in range(n)]
    trhs = [_bdot(t[i], rhs[i]) for i in range(n)]
    return ([rhs[i] + trhs[i] for i in range(n)], [q[i] * egc[i] for i in range(n)],
            [qk[i] * decay[i] for i in range(n)])


def _gdn_prompt_kernel(q_ref, k_ref, v_ref, z_ref, bg_ref, wq_ref, wk_ref, wv_ref, ng_ref,
                       o_ref, s_out_ref, qwin, kwin, vwin, s_ref, *, seq_len, rows, chunk, g_heads, n_sq):
    h = pl.program_id(1)
    c = pl.program_id(2)
    nc = pl.num_programs(2)
    dk = GDN_HEAD
    nb = rows // chunk
    lead = 8

    @pl.when(c == 0)
    def _():
        s_ref[...] = jnp.zeros_like(s_ref)
        for win in (qwin, kwin, vwin):
            win[pl.ds(0, lead), :] = jnp.zeros((lead, win.shape[1]), F32)

    row = lax.broadcasted_iota(jnp.int32, (rows, 1), 0)
    valid = (c * rows + row) < seq_len

    def conv(win, x_ref, w_ref):
        win[pl.ds(lead, rows), :] = x_ref[0]
        acc = win[pl.ds(lead - 3, rows), :] * w_ref[0:1, :]
        for tap in range(1, GDN_CONV_TAPS):
            acc = acc + win[pl.ds(lead - 3 + tap, rows), :] * w_ref[tap:tap + 1, :]
        win[pl.ds(lead - 3, 3), :] = win[pl.ds(lead + rows - 3, 3), :]
        return jnp.where(valid, _silu(acc), 0.0)

    qc = conv(qwin, q_ref, wq_ref)
    kc = conv(kwin, k_ref, wk_ref)
    vc = conv(vwin, v_ref, wv_ref)
    bg = jnp.where(valid, bg_ref[0], 0.0)
    nh = bg.shape[1] // 2

    ri = lax.broadcasted_iota(jnp.int32, (rows, rows), 0)
    ci = lax.broadcasted_iota(jnp.int32, (rows, rows), 1)
    same = (ri // chunk) == (ci // chunk)
    tril = same & (ri >= ci)
    strict = same & (ri > ci)
    gc_all = _exact_left_dot(tril, bg)

    nv = 2 * g_heads
    qs = [_l2norm(qc[:, gi * dk:(gi + 1) * dk]) * (dk ** -0.5) for gi in range(g_heads)]
    ks = [_l2norm(kc[:, gi * dk:(gi + 1) * dk]) for gi in range(g_heads)]
    kks = [_bdot_nt(x, x) for x in ks]
    qks = [_bdot_nt(x, y) for x, y in zip(qs, ks)]
    per_v = lambda xs: [xs[sv // 2] for sv in range(nv)]
    q, k = per_v(qs), per_v(ks)
    v = [vc[:, sv * dk:(sv + 1) * dk] for sv in range(nv)]
    beta = [_col(bg, 2 * h * g_heads + sv) for sv in range(nv)]
    gc = [_col(gc_all, nh + 2 * h * g_heads + sv) for sv in range(nv)]
    uw, qg, qkd = _delta_wy(per_v(kks), per_v(qks), k, q, v, beta, gc, tril, strict, n_sq)
    ow = [_bdot(x, y) for x, y in zip(qkd, uw)]
    q_eff = [qg[i] - ow[i][:, dk:] for i in range(nv)]
    s = [s_ref[sv] for sv in range(nv)]
    o_parts = [[] for _ in range(nv)]
    for j in range(nb):
        sl = slice(j * chunk, (j + 1) * chunk)
        g_end = [g[(j + 1) * chunk - 1:(j + 1) * chunk, :] for g in gc]
        kd = [k[i][sl] * jnp.exp(g_end[i] - gc[i][sl]) for i in range(nv)]
        bn = [_bdot_tn(kd[i], uw[i][sl]) for i in range(nv)]
        qs_j = [_bdot(q_eff[i][sl], s[i]) for i in range(nv)]
        ns_j = [_bdot(bn[i][:, dk:], s[i]) for i in range(nv)]
        for i in range(nv):
            o_parts[i].append(qs_j[i] + ow[i][sl, :dk])
            s[i] = s[i] * jnp.exp(g_end[i]) - ns_j[i] + bn[i][:, :dk]
    outs = []
    for sv in range(nv):
        s_ref[sv] = s[sv]
        o = o_parts[sv][0] if nb == 1 else jnp.concatenate(o_parts[sv], axis=0)
        z = z_ref[0, :, sv * dk:(sv + 1) * dk]
        outs.append(o * _rms_scale(o) * ng_ref[...] * _silu(z))
    o_ref[0] = jnp.concatenate(outs, axis=1).astype(o_ref.dtype)

    @pl.when(c == nc - 1)
    def _():
        s_out_ref[0] = s_ref[...]


def gdn_prompt(raw, bg, conv_w, norm_g, rows, chunk, g_heads):
    b, t, _ = raw.shape
    nh = bg.shape[-1] // 2
    dk = GDN_HEAD
    n_qk = nh // 2
    qk_dim = n_qk * dk
    gq, gv = g_heads * dk, 2 * g_heads * dk
    n_hblk = n_qk // g_heads
    nc = pl.cdiv(t, rows)
    n_sq = int(math.log2(chunk)) - 1
    assert 2 ** (n_sq + 1) == chunk and n_qk % g_heads == 0 and rows % chunk == 0
    blk = lambda width, off: pl.BlockSpec((1, rows, width), lambda bi, h, c: (bi, c, off + h))
    wblk = lambda width, off: pl.BlockSpec((GDN_CONV_TAPS, width), lambda bi, h, c: (0, off + h))
    kernel = functools.partial(_gdn_prompt_kernel, seq_len=t, rows=rows, chunk=chunk, g_heads=g_heads,
                               n_sq=n_sq)
    return pl.pallas_call(
        kernel,
        grid=(b, n_hblk, nc),
        in_specs=[
            blk(gq, 0), blk(gq, n_hblk), blk(gv, n_hblk), blk(gv, 2 * n_hblk),
            pl.BlockSpec((1, rows, 2 * nh), lambda bi, h, c: (bi, c, 0)),
            wblk(gq, 0), wblk(gq, n_hblk), wblk(gv, n_hblk),
            pl.BlockSpec((1, dk), lambda bi, h, c: (0, 0)),
        ],
        out_specs=[
            pl.BlockSpec((1, rows, gv), lambda bi, h, c: (bi, c, h)),
            pl.BlockSpec((1, 2 * g_heads, dk, dk), lambda bi, h, c: (bi, h, 0, 0)),
        ],
        out_shape=[jax.ShapeDtypeStruct((b, t, 2 * qk_dim), BF16),
                   jax.ShapeDtypeStruct((b, nh, dk, dk), F32)],
        scratch_shapes=[pltpu.VMEM((8 + rows, gq), F32), pltpu.VMEM((8 + rows, gq), F32),
                        pltpu.VMEM((8 + rows, gv), F32), pltpu.VMEM((2 * g_heads, dk, dk), F32)],
        compiler_params=_params("parallel", "parallel", "arbitrary"),
        name="gdn_prompt",
    )(raw, raw, raw, raw, bg, conv_w, conv_w, conv_w, norm_g.reshape(1, dk))


def _gdn_sample_kernel(q_ref, k_ref, v_ref, z_ref, bg_ref, cq_ref, ck_ref, cv_ref, wq_ref, wk_ref, wv_ref,
                       ng_ref, s_in_ref, o_ref, s_out_ref, qwin, kwin, vwin, *, n_sq):
    h = pl.program_id(0)
    tb, t, dk = q_ref.shape
    r = tb * t
    lead = 8

    def conv(win, x_ref, c_ref, w_ref):
        win[:, pl.ds(lead - 3, 3), :] = c_ref[...]
        win[:, pl.ds(lead, t), :] = x_ref[...]
        acc = win[:, pl.ds(lead - 3, t), :] * w_ref[0:1, :]
        for i in range(1, 4):
            acc = acc + win[:, pl.ds(lead - 3 + i, t), :] * w_ref[i:i + 1, :]
        return _silu(acc).reshape(r, x_ref.shape[-1])

    qc = conv(qwin, q_ref, cq_ref, wq_ref)
    kc = conv(kwin, k_ref, ck_ref, wk_ref)
    vc = conv(vwin, v_ref, cv_ref, wv_ref)
    bg = bg_ref[...].reshape(r, bg_ref.shape[-1])
    nh = bg.shape[1] // 2

    ri = lax.broadcasted_iota(jnp.int32, (r, r), 0)
    ci = lax.broadcasted_iota(jnp.int32, (r, r), 1)
    same = (ri // t) == (ci // t)
    tril = same & (ri >= ci)
    strict = same & (ri > ci)
    gc_all = _exact_left_dot(tril, bg)
    g_sum = jnp.sum(bg_ref[...], axis=1, keepdims=True)

    q = _l2norm(qc) * (dk ** -0.5)
    k = _l2norm(kc)
    kk = _bdot_nt(k, k)
    qk = _bdot_nt(q, k)
    bdims = (((2,), (1,)), ((0,), (0,)))
    two = range(2)
    v = [vc[:, e * dk:(e + 1) * dk] for e in two]
    beta = [_col(bg, 2 * h + e) for e in two]
    gc = [_col(gc_all, nh + 2 * h + e) for e in two]
    uw, qg, qkd = _delta_wy([kk, kk], [qk, qk], [k, k], [q, q], v, beta, gc, tril, strict, n_sq)
    s = [s_in_ref[:, e] for e in two]
    wq3 = [jnp.concatenate([uw[e][:, dk:].reshape(tb, t, dk), qg[e].reshape(tb, t, dk)], axis=1).astype(BF16)
           for e in two]
    ws_qs = [lax.dot_general(wq3[e], s[e].astype(BF16), bdims, preferred_element_type=F32) for e in two]
    v_new = [uw[e][:, :dk] - ws_qs[e][:, :t].reshape(r, dk) for e in two]
    qkv = [_bdot(qkd[e], v_new[e]) for e in two]
    lane3 = lax.broadcasted_iota(jnp.int32, g_sum.shape, 2)
    g_last = [jnp.sum(jnp.where(lane3 == nh + 2 * h + e, g_sum, 0.0), axis=-1, keepdims=True) for e in two]
    kd = [(k.reshape(tb, t, dk) * jnp.exp(g_last[e] - gc[e].reshape(tb, t, 1))).astype(BF16) for e in two]
    upd = [lax.dot_general(kd[e], v_new[e].reshape(tb, t, dk).astype(BF16), (((1,), (1,)), ((0,), (0,))),
                           preferred_element_type=F32) for e in two]
    for e in two:
        s_out_ref[:, e] = s[e] * jnp.exp(g_last[e]) + upd[e]
        o = ws_qs[e][:, t:].reshape(r, dk) + qkv[e]
        z = z_ref[:, :, e * dk:(e + 1) * dk].reshape(r, dk)
        o_ref[:, :, e * dk:(e + 1) * dk] = (o * _rms_scale(o) * ng_ref[...] * _silu(z)).reshape(tb, t, dk)


def gdn_sample(raw, bg, conv_state, ssm_state, conv_w, norm_g, tb):
    b, t, _ = raw.shape
    nh = bg.shape[-1] // 2
    dk = GDN_HEAD
    n_qk = nh // 2
    n_sq = int(math.log2(t)) - 1
    assert 2 ** (n_sq + 1) == t and b % tb == 0 and t == 8
    blk = lambda width, off: pl.BlockSpec((tb, t, width), lambda h, i: (i, 0, off + h))
    cblk = lambda width, off: pl.BlockSpec((tb, 3, width), lambda h, i: (i, 0, off + h))
    wblk = lambda width, off: pl.BlockSpec((4, width), lambda h, i: (0, off + h))
    sblk = pl.BlockSpec((tb, 2, dk, dk), lambda h, i: (i, h, 0, 0))
    return pl.pallas_call(
        functools.partial(_gdn_sample_kernel, n_sq=n_sq),
        grid=(n_qk, b // tb),
        in_specs=[
            blk(dk, 0), blk(dk, n_qk), blk(2 * dk, n_qk), blk(2 * dk, 2 * n_qk),
            pl.BlockSpec((tb, t, 2 * nh), lambda h, i: (i, 0, 0)),
            cblk(dk, 0), cblk(dk, n_qk), cblk(2 * dk, n_qk),
            wblk(dk, 0), wblk(dk, n_qk), wblk(2 * dk, n_qk),
            pl.BlockSpec((1, dk), lambda h, i: (0, 0)),
            sblk,
        ],
        out_specs=[pl.BlockSpec((tb, t, 2 * dk), lambda h, i: (i, 0, h)), sblk],
        out_shape=[jax.ShapeDtypeStruct((b, t, nh * dk), F32),
                   jax.ShapeDtypeStruct(ssm_state.shape, F32)],
        scratch_shapes=[pltpu.VMEM((tb, 8 + t, dk), F32), pltpu.VMEM((tb, 8 + t, dk), F32),
                        pltpu.VMEM((tb, 8 + t, 2 * dk), F32)],
        compiler_params=_params("parallel", "parallel"),
        name="gdn_sample",
    )(raw, raw, raw, raw, bg, conv_state, conv_state, conv_state, conv_w, conv_w, conv_w,
      norm_g.reshape(1, dk), ssm_state)


def _t5_bucket(dist):
    max_exact = NUM_BUCKETS // 2
    n = jnp.maximum(dist, 0)
    log_ratio = jnp.log(jnp.maximum(n, 1).astype(F32) * (1.0 / max_exact)) * (1.0 / math.log(MAX_DISTANCE / max_exact))
    large = jnp.minimum(max_exact + (log_ratio * (NUM_BUCKETS - max_exact)).astype(jnp.int32), NUM_BUCKETS - 1)
    return jnp.where(n < max_exact, n, large)


def _prompt_bias_kernel(rb_ref, o_ref, *, tq, tk, n_meta):
    ty = pl.program_id(0)
    m = pl.program_id(1)
    ii = lax.broadcasted_iota(jnp.int32, (tq, tk), 0)
    jj = lax.broadcasted_iota(jnp.int32, (tq, tk), 1)
    dist = n_meta + (ty - 1) * tk + ii - jj
    bucket = _t5_bucket(dist)
    tile = jnp.full((tq, tk), NEG_BIG, F32)
    for bkt in range(NUM_BUCKETS):
        tile = jnp.where((bucket == bkt) & (dist >= 0), rb_ref[bkt, m], tile)
    o_ref[0, 0] = tile


def prompt_bias_tiles(rel_bias, tq, tk, n_meta):
    n_maps = rel_bias.shape[1]
    return pl.pallas_call(
        functools.partial(_prompt_bias_kernel, tq=tq, tk=tk, n_meta=n_meta),
        grid=(3, n_maps),
        in_specs=[pl.BlockSpec(memory_space=pltpu.SMEM)],
        out_specs=pl.BlockSpec((1, 1, tq, tk), lambda ty, m: (ty, m, 0, 0)),
        out_shape=jax.ShapeDtypeStruct((3, n_maps, tq, tk), F32),
        compiler_params=_params("parallel", "parallel"),
        name="prompt_bias",
    )(rel_bias.astype(F32))


def _lambda_of(lv_ref, lam_init):
    lv = lv_ref[...].astype(F32)
    a = jnp.sum(lv[0:1] * lv[1:2], axis=-1, keepdims=True)
    b = jnp.sum(lv[2:3] * lv[3:4], axis=-1, keepdims=True)
    return jnp.exp(a) - jnp.exp(b) + lam_init


def _attn_prompt_kernel(qi_ref, kj_ref, q_ref, k_ref, v_ref, bias_ref, rb_ref, lv_ref, sub_ref, o_ref,
                        m_ref, l_ref, acc_ref, *, n_keys, nk, tk, hp, lam_init, scale):
    h = pl.program_id(1)
    step = pl.program_id(2)
    qi = qi_ref[step]
    kj = kj_ref[step]
    dh = DIFF_HEAD
    hw = 2 * dh

    @pl.when(kj == 0)
    def _():
        m_ref[...] = jnp.full_like(m_ref, NEG_BIG)
        l_ref[...] = jnp.zeros_like(l_ref)
        acc_ref[...] = jnp.zeros_like(acc_ref)

    krow = lax.broadcasted_iota(jnp.int32, (tk, 1), 0)
    kvalid = (kj * tk + krow) < n_keys
    k = jnp.where(kvalid, k_ref[0], 0.0).astype(BF16)
    v = jnp.where(kvalid, v_ref[0], 0.0).astype(BF16)
    near = (qi - kj) <= 1
    lanes = m_ref.shape[-1]
    wide = lambda x, n: jnp.concatenate([x] * (n // lanes), axis=1)
    ones = jnp.ones((tk, lanes), BF16)
    maps = range(2 * hp)
    s = [_bdot_nt((q_ref[0, :, mi * dh:(mi + 1) * dh] * scale).astype(BF16), k[:, mi * dh:(mi + 1) * dh])
         for mi in maps]
    ps, corrs = [], []
    for mi in maps:
        far_bias = rb_ref[NUM_BUCKETS - 1, 2 * hp * h + mi]
        sb = s[mi] + jnp.where(near, bias_ref[0, mi], far_bias)
        m_old = m_ref[mi]
        m_new = jnp.maximum(m_old, jnp.max(sb, axis=-1, keepdims=True))
        m_ref[mi] = m_new
        corrs.append(jnp.exp(m_old - m_new))
        ps.append(jnp.exp(sb - wide(m_new, tk)).astype(BF16))
    sums = [jnp.dot(ps[mi], ones, preferred_element_type=F32) for mi in maps]
    pvs = [jnp.dot(ps[mi], v[:, (mi // 2) * hw:(mi // 2 + 1) * hw], preferred_element_type=F32) for mi in maps]
    for mi in maps:
        l_ref[mi] = l_ref[mi] * corrs[mi] + sums[mi]
        acc_ref[mi] = acc_ref[mi] * wide(corrs[mi], hw) + pvs[mi]

    @pl.when(kj == jnp.minimum(qi + 1, nk - 1))
    def _():
        lam = _lambda_of(lv_ref, lam_init)
        for hh in range(hp):
            o = (acc_ref[2 * hh] / wide(l_ref[2 * hh], hw)
                 - lam * (acc_ref[2 * hh + 1] / wide(l_ref[2 * hh + 1], hw)))
            o_ref[0, :, hh * hw:(hh + 1) * hw] = (
                o * _rms_scale(o) * sub_ref[...] * (1.0 - lam_init)).astype(o_ref.dtype)


def attn_prompt(q, k, v, bias_tiles, rel_bias, lam_vecs, subln, lam_init, tq, tk, hp):
    b, s, d = q.shape
    n_keys = k.shape[1]
    n_heads = d // (2 * DIFF_HEAD)
    hw = 2 * DIFF_HEAD
    bw = hp * hw
    nq, nk = s // tq, pl.cdiv(n_keys, tk)
    assert s % tq == 0 and tq == tk and N_META <= tk and n_heads % hp == 0
    assert N_META + 2 * tk - (tk - 1) >= MAX_DISTANCE
    pairs = [(qi, kj) for qi in range(nq) for kj in range(min(qi + 1, nk - 1) + 1)]
    qi_tab = jnp.asarray([p[0] for p in pairs], jnp.int32)
    kj_tab = jnp.asarray([p[1] for p in pairs], jnp.int32)

    qmap = lambda bi, h, st, qt, kt: (bi, qt[st], h)
    kvmap = lambda bi, h, st, qt, kt: (bi, kt[st], h)
    bias_map = lambda bi, h, st, qt, kt: (jnp.clip(qt[st] - kt[st] + 1, 0, 2), h, 0, 0)
    const2 = lambda bi, h, st, qt, kt: (0, 0)
    kernel = functools.partial(_attn_prompt_kernel, n_keys=n_keys, nk=nk, tk=tk, hp=hp, lam_init=lam_init,
                               scale=DIFF_HEAD ** -0.5)
    grid_spec = pltpu.PrefetchScalarGridSpec(
        num_scalar_prefetch=2,
        grid=(b, n_heads // hp, len(pairs)),
        in_specs=[
            pl.BlockSpec((1, tq, bw), qmap),
            pl.BlockSpec((1, tk, bw), kvmap),
            pl.BlockSpec((1, tk, bw), kvmap),
            pl.BlockSpec((1, 2 * hp, tq, tk), bias_map),
            pl.BlockSpec(memory_space=pltpu.SMEM),
            pl.BlockSpec(lam_vecs.shape, const2),
            pl.BlockSpec((1, hw), const2),
        ],
        out_specs=pl.BlockSpec((1, tq, bw), qmap),
        scratch_shapes=[pltpu.VMEM((2 * hp, tq, LANES), F32), pltpu.VMEM((2 * hp, tq, LANES), F32),
                        pltpu.VMEM((2 * hp, tq, hw), F32)],
    )
    return pl.pallas_call(
        kernel,
        grid_spec=grid_spec,
        out_shape=jax.ShapeDtypeStruct((b, s, d), BF16),
        compiler_params=_params("parallel", "parallel", "arbitrary"),
        name="attn_prompt",
    )(qi_tab, kj_tab, q, k, v, bias_tiles, rel_bias.astype(F32), lam_vecs, subln.reshape(1, hw))


def _sample_bias_kernel(rb_ref, o_ref, *, t, page, past_len):
    r = o_ref.shape[1]
    ri = lax.broadcasted_iota(jnp.int32, (r, page), 0)
    jj = lax.broadcasted_iota(jnp.int32, (r, page), 1)
    tq = ri % t
    far = rb_ref[:, NUM_BUCKETS - 1:NUM_BUCKETS]
    o_ref[0] = jnp.broadcast_to(far, (r, page))
    for ty, dist, ok in ((1, page + tq - jj, None), (2, tq - jj, (jj <= tq) & (jj < t))):
        bucket = _t5_bucket(dist)
        tile = jnp.full((r, page), NEG_BIG, F32)
        for bkt in range(NUM_BUCKETS):
            hit = (bucket == bkt) if ok is None else ((bucket == bkt) & ok)
            tile = jnp.where(hit, rb_ref[:, bkt:bkt + 1], tile)
        o_ref[ty] = tile
    del past_len


def sample_bias_tiles(rel_bias, t, page, past_len):
    n_maps = rel_bias.shape[1]
    assert past_len % page == 0 and past_len - page >= MAX_DISTANCE
    rb_rows = jnp.repeat(rel_bias.astype(F32).T, t, axis=0)
    return pl.pallas_call(
        functools.partial(_sample_bias_kernel, t=t, page=page, past_len=past_len),
        out_shape=jax.ShapeDtypeStruct((3, n_maps * t, page), F32),
        compiler_params=pltpu.CompilerParams(vmem_limit_bytes=VMEM_LIMIT),
        name="sample_bias",
    )(rb_rows)


def _attn_sample_kernel(*refs, lam_init, scale, page, pps):
    pt_ref, q_ref = refs[0], refs[1]
    kc_refs, vlo_refs, vhi_refs = refs[2:2 + pps], refs[2 + pps:2 + 2 * pps], refs[2 + 2 * pps:2 + 3 * pps]
    (kn_ref, vn_ref, far_ref, bias_ref, nbias_ref, lv_ref, sub_ref, o_ref, m_ref, l_ref, acc_ref) = refs[2 + 3 * pps:]
    del pt_ref
    p = pl.program_id(1)
    n_steps = pl.num_programs(1)
    t, d = q_ref.shape[1], q_ref.shape[2]
    dh = DIFF_HEAD
    hw = 2 * dh
    n_maps = d // dh
    n_heads = n_maps // 2

    @pl.when(p == 0)
    def _():
        m_ref[...] = jnp.full_like(m_ref, NEG_BIG)
        l_ref[...] = jnp.zeros_like(l_ref)
        acc_ref[...] = jnp.zeros_like(acc_ref)

    def update(k_ofs, v_ofs, biases):
        s = jnp.concatenate(
            [jnp.concatenate([_bdot_nt(q_ref[0, :, m * dh:(m + 1) * dh] * scale, k_of(m)) for m in range(n_maps)],
                             axis=0) + bias for k_of, bias in zip(k_ofs, biases)], axis=1)
        m_old = m_ref[...]
        m_new = jnp.maximum(m_old, jnp.max(s, axis=-1, keepdims=True))
        corr = jnp.exp(m_old - m_new)
        pr = jnp.exp(s - m_new)
        l_ref[...] = l_ref[...] * corr + jnp.sum(pr, axis=-1, keepdims=True)
        pv = jnp.concatenate(
            [_bdot(pr[2 * h * t:(2 * h + 2) * t], jnp.concatenate([v_of(h) for v_of in v_ofs], axis=0))
             for h in range(n_heads)], axis=0)
        acc_ref[...] = acc_ref[...] * corr + pv
        m_ref[...] = m_new

    strided_k = lambda ref: (lambda m: ref[pl.ds(m, page, stride=n_maps), :])
    strided_v = lambda lo, hi: (lambda h: jnp.concatenate([lo[pl.ds(h, page, stride=n_heads), :],
                                                           hi[pl.ds(h, page, stride=n_heads), :]], axis=1))
    update([strided_k(r) for r in kc_refs], [strided_v(lo, hi) for lo, hi in zip(vlo_refs, vhi_refs)],
           [far_ref[0]] * (pps - 1) + [bias_ref[0]])

    @pl.when(p == n_steps - 1)
    def _():
        kn, vn = kn_ref[0], vn_ref[0]
        zk = jnp.zeros((page - t, dh), F32)
        zv = jnp.zeros((page - t, hw), F32)
        update([lambda m: jnp.concatenate([kn[:, m * dh:(m + 1) * dh], zk], axis=0)],
               [lambda h: jnp.concatenate([vn[:, h * hw:(h + 1) * hw], zv], axis=0)], [nbias_ref[0]])
        lam = _lambda_of(lv_ref, lam_init)
        for h in range(n_heads):
            r0 = 2 * h * t
            o0 = acc_ref[pl.ds(r0, t), :] / l_ref[pl.ds(r0, t), :]
            o1 = acc_ref[pl.ds(r0 + t, t), :] / l_ref[pl.ds(r0 + t, t), :]
            o = o0 - lam * o1
            o_ref[0, :, h * hw:(h + 1) * hw] = o * _rms_scale(o) * sub_ref[...] * (1.0 - lam_init)


def attn_sample(q, k_new, v_new, cache_k, cache_v, page_table, bias_tiles, lam_vecs, subln, lam_init, pps):
    b, t, d = q.shape
    n_pages = page_table.shape[1]
    dh = DIFF_HEAD
    hw = 2 * dh
    n_maps = d // dh
    page = cache_k.shape[1] // n_maps
    r = n_maps * t
    n_steps = n_pages // pps
    assert r == bias_tiles.shape[1] and page == bias_tiles.shape[2] and cache_v.shape[1] == page * n_maps // 2
    assert n_pages % pps == 0

    def cache_map(i, half):
        return lambda bi, p, pt: (pt[bi, p * pps + i], 0, half)

    row = lambda bi, p, pt: (bi, 0, 0)
    tile = lambda index: pl.BlockSpec((1, r, page), index)
    grid_spec = pltpu.PrefetchScalarGridSpec(
        num_scalar_prefetch=1,
        grid=(b, n_steps),
        in_specs=[pl.BlockSpec((1, t, d), row)]
        + [pl.BlockSpec((None, page * n_maps, dh), cache_map(i, 0)) for i in range(pps)]
        + [pl.BlockSpec((None, page * n_maps // 2, dh), cache_map(i, 0)) for i in range(pps)]
        + [pl.BlockSpec((None, page * n_maps // 2, dh), cache_map(i, 1)) for i in range(pps)]
        + [
            pl.BlockSpec((1, t, d), row),
            pl.BlockSpec((1, t, d), row),
            tile(lambda bi, p, pt: (0, 0, 0)),
            tile(lambda bi, p, pt: (jnp.where(p == n_steps - 1, 1, 0), 0, 0)),
            tile(lambda bi, p, pt: (2, 0, 0)),
            pl.BlockSpec(lam_vecs.shape, lambda bi, p, pt: (0, 0)),
            pl.BlockSpec((1, hw), lambda bi, p, pt: (0, 0)),
        ],
        out_specs=pl.BlockSpec((1, t, d), row),
        scratch_shapes=[pltpu.VMEM((r, 1), F32), pltpu.VMEM((r, 1), F32), pltpu.VMEM((r, hw), F32)],
    )
    return pl.pallas_call(
        functools.partial(_attn_sample_kernel, lam_init=lam_init, scale=dh ** -0.5, page=page, pps=pps),
        grid_spec=grid_spec,
        out_shape=jax.ShapeDtypeStruct((b, t, d), F32),
        compiler_params=_params("parallel", "arbitrary"),
        name="attn_sample",
    )(page_table, q, *([cache_k] * pps), *([cache_v] * (2 * pps)), k_new, v_new, bias_tiles, bias_tiles,
      bias_tiles, lam_vecs, subln.reshape(1, hw))


TN = 512
TN_MM = 1024
ATTN_TILE = 256
ATTN_HEADS_PER_STEP = 8
GDN_CHUNK = 64
GDN_ROWS = 128
GDN_HEADS_PER_STEP = 8
GDN_SEQS_PER_STEP = 16
PAGES_PER_STEP = 8


def _row_tile(m, target):
    best = 16
    for t in range(16, target + 1, 16):
        if m % t == 0:
            best = t
    assert m % best == 0
    return best


def _trunk(h, n_seq, n_drop, gdn_fn, attn_fn, kv_fn, ln_gain, ffn_w_in, ffn_w_out, gdn_w_out, kv_norm,
           diff_w_q, diff_w_out, final_norm, n_a, tm_target):
    depth = ln_gain.shape[0]
    d = h.shape[-1]
    tm = _row_tile(h.shape[0], tm_target)
    hn = rmsnorm(h, ln_gain[0, 0], tm)
    conv_states, ssm_states, kv = [], [], None
    y = None
    row0 = 0
    for i in range(depth):
        rows = h.shape[0] if h.ndim == 2 else h.shape[0] * (h.shape[1] - row0)
        tm = _row_tile(rows, tm_target)
        act = ffn_in(hn, ffn_w_in, i, 0, tm, TN, row0)
        h, (hn,) = mm_res_norm(act, ffn_w_out, (i, 0), h, ln_gain[i, 1][None], [BF16], 0.5, tm_target, row0)
        row0 = 0
        if i < n_a:
            mix, conv_new, ssm_new = gdn_fn(i, hn, tm)
            conv_states.append(conv_new)
            ssm_states.append(ssm_new)
            h, (hn,) = mm_res_norm(mix, gdn_w_out, (i,), h, ln_gain[i, 2][None], [BF16], 1.0, tm_target)
        else:
            j = i - n_a
            q = mm(hn, diff_w_q, (j,), 0, d, tm, TN_MM)
            att = attn_fn(j, i, q, kv)
            h, (hn,) = mm_res_norm(att, diff_w_out, (j,), h, ln_gain[i, 2][None], [BF16], 1.0, tm_target)
        act = ffn_in(hn, ffn_w_in, i, 1, tm, TN)
        if i == depth - 1:
            _, (y,) = mm_res_norm(act, ffn_w_out, (i, 1), h, final_norm[None], [F32], 0.5, tm_target)
        elif i == n_a - 1:
            gains = jnp.stack([kv_norm, ln_gain[i + 1, 0]])
            h, (hkv, hn) = mm_res_norm(act, ffn_w_out, (i, 1), h, gains, [BF16, BF16], 0.5, tm_target)
            kv = kv_fn(hkv, tm)
            if n_drop:
                t = h.shape[0] // n_seq
                assert (t - n_drop) % 16 == 0
                h, hn, row0 = h.reshape(n_seq, t, d), hn.reshape(n_seq, t, d), n_drop
        else:
            h, (hn,) = mm_res_norm(act, ffn_w_out, (i, 1), h, ln_gain[i + 1, 0][None], [BF16], 0.5, tm_target)
    return y, kv, jnp.stack(conv_states), jnp.stack(ssm_states)


def kernel(x_prompt, x_sample, cache_k, cache_v, state_conv, state_ssm, page_table, meta_tokens, ln_gain,
           ffn_w_in, ffn_w_out, gdn_w_in, gdn_conv, gdn_a_log, gdn_dt_bias, gdn_norm, gdn_w_out, kv_norm,
           w_kv, diff_w_q, diff_lambda, diff_subln, diff_w_out, rel_bias, final_norm):
    b, seq, d = x_prompt.shape
    db, dseq, _ = x_sample.shape
    n_a = gdn_w_in.shape[0]
    n_heads_v = gdn_a_log.shape[1]
    qk_dim = (n_heads_v // 2) * GDN_HEAD
    conv_dim = gdn_conv.shape[-1]
    main_cols = conv_dim + n_heads_v * GDN_HEAD
    n_meta = meta_tokens.shape[0]
    t_prompt = n_meta + seq
    kv_cols = w_kv.shape[1] // 2
    n_maps = rel_bias.shape[1]
    page = cache_k.shape[1]
    past_len = page_table.shape[1] * page
    assert n_meta == N_META and conv_dim == 2 * qk_dim + n_heads_v * GDN_HEAD
    lam_init = lambda layer: 0.8 - 0.6 * math.exp(-0.3 * layer)

    def gdn_proj(i, hn, tm):
        raw = mm(hn, gdn_w_in, (i,), 0, main_cols, tm, TN_MM)
        bg = gdn_ba(hn, gdn_w_in, i, main_cols, gdn_a_log[i], gdn_dt_bias[i], tm)
        return raw, bg

    def kv_proj(hkv, tm):
        return (mm(hkv, w_kv, (), 0, kv_cols, tm, TN_MM), mm(hkv, w_kv, (), kv_cols, kv_cols, tm, TN_MM))

    def gdn_prompt_fn(i, hn, tm):
        raw, bg = gdn_proj(i, hn, tm)
        raw3 = raw.reshape(b, t_prompt, main_cols)
        o, ssm = gdn_prompt(raw3, bg.reshape(b, t_prompt, -1), gdn_conv[i], gdn_norm[i], GDN_ROWS, GDN_CHUNK,
                            GDN_HEADS_PER_STEP)
        return o.reshape(b * t_prompt, -1), raw3[:, t_prompt - GDN_CONV_TAPS + 1:, :conv_dim], ssm

    prompt_bias = prompt_bias_tiles(rel_bias, ATTN_TILE, ATTN_TILE, n_meta)

    def attn_prompt_fn(j, layer, q, kv):
        k, v = kv
        o = attn_prompt(q.reshape(b, seq, d), k.reshape(b, t_prompt, kv_cols), v.reshape(b, t_prompt, kv_cols),
                        prompt_bias, rel_bias, diff_lambda[j], diff_subln[j], lam_init(layer), ATTN_TILE, ATTN_TILE,
                        ATTN_HEADS_PER_STEP)
        return o.reshape(b * seq, d)

    meta = jnp.broadcast_to(meta_tokens.astype(x_prompt.dtype)[None], (b, n_meta, d))
    h_prompt = jnp.concatenate([meta, x_prompt], axis=1).reshape(b * t_prompt, d)
    shared = (ln_gain, ffn_w_in, ffn_w_out.astype(BF16), gdn_w_out.astype(BF16), kv_norm, diff_w_q,
              diff_w_out.astype(BF16), final_norm, n_a)
    y_p, (k_p, v_p), conv_p, ssm_p = _trunk(h_prompt, b, n_meta, gdn_prompt_fn, attn_prompt_fn, kv_proj,
                                            *shared, tm_target=1408)

    def gdn_sample_fn(i, hn, tm):
        raw, bg = gdn_proj(i, hn, tm)
        raw3 = raw.reshape(db, dseq, main_cols)
        o, ssm = gdn_sample(raw3, bg.reshape(db, dseq, -1), state_conv[i], state_ssm[i], gdn_conv[i],
                            gdn_norm[i], GDN_SEQS_PER_STEP)
        return o.reshape(db * dseq, -1), raw3[:, dseq - 3:, :conv_dim], ssm

    sample_bias = sample_bias_tiles(rel_bias, dseq, page, past_len)
    cache_k2 = cache_k.reshape(cache_k.shape[0], page * cache_k.shape[2], cache_k.shape[3])
    cache_v2 = cache_v.reshape(cache_v.shape[0], page * cache_v.shape[2], cache_v.shape[3])

    def attn_sample_fn(j, layer, q, kv):
        k, v = kv
        r3 = lambda x: x.reshape(db, dseq, kv_cols)
        o = attn_sample(r3(q), r3(k), r3(v), cache_k2, cache_v2, page_table, sample_bias, diff_lambda[j],
                        diff_subln[j], lam_init(layer), PAGES_PER_STEP)
        return o.reshape(db * dseq, d)

    y_s, (k_s, v_s), conv_s, ssm_s = _trunk(x_sample.reshape(db * dseq, d), db, 0, gdn_sample_fn,
                                            attn_sample_fn, kv_proj, *shared, tm_target=1024)

    hd = DIFF_HEAD
    return (y_p.reshape(b, seq, d), y_s.reshape(db, dseq, d),
            k_p.reshape(b, t_prompt, n_maps, hd), v_p.reshape(b, t_prompt, n_maps // 2, 2 * hd),
            conv_p, ssm_p,
            k_s.reshape(db, dseq, n_maps, hd), v_s.reshape(db, dseq, n_maps // 2, 2 * hd),
            conv_s, ssm_s)
```

```python
import functools
import math

import jax
import jax.numpy as jnp
from jax import lax
from jax.experimental import pallas as pl
from jax.experimental.pallas import tpu as pltpu

F32 = jnp.float32
BF16 = jnp.bfloat16

EPS = 1e-6
L2_EPS = 1e-6
N_META = 16
GDN_HEAD = 128
GDN_CONV_TAPS = 4
DIFF_HEAD = 128
NUM_BUCKETS = 32
MAX_DISTANCE = 128
NEG_BIG = -1e30
LANES = 128
VMEM_LIMIT = 56 * 1024 * 1024
VMEM_BUDGET = 46 * 1024 * 1024
RES_TM_MAX = 768


def _params(*sem):
    return pltpu.CompilerParams(dimension_semantics=sem, vmem_limit_bytes=VMEM_LIMIT)


def _bdot(a, b):
    return jnp.dot(a.astype(BF16), b.astype(BF16), preferred_element_type=F32)


def _bdot_nt(a, b):
    return lax.dot_general(a.astype(BF16), b.astype(BF16), (((1,), (1,)), ((), ())),
                           preferred_element_type=F32)


def _bdot_tn(a, b):
    return lax.dot_general(a.astype(BF16), b.astype(BF16), (((0,), (0,)), ((), ())),
                           preferred_element_type=F32)


def _hdot(a, b):
    return jnp.dot(a, b, precision=lax.Precision.HIGHEST, preferred_element_type=F32)


def _split3(x):
    hi = x.astype(BF16)
    r1 = x - hi.astype(F32)
    mid = r1.astype(BF16)
    lo = (r1 - mid.astype(F32)).astype(BF16)
    return hi, mid, lo


def _exact_left_dot(mask01, x):
    m = mask01.astype(BF16)
    hi, mid, lo = _split3(x)
    d = functools.partial(jnp.dot, preferred_element_type=F32)
    return d(m, hi) + d(m, mid) + d(m, lo)


def _silu(x):
    return x * jax.nn.sigmoid(x)


def _rms_scale(x):
    return lax.rsqrt(jnp.mean(x * x, axis=-1, keepdims=True) + EPS)


def _rmsnorm_kernel(x_ref, g_ref, o_ref):
    x = x_ref[...].astype(F32)
    o_ref[...] = (x * _rms_scale(x) * g_ref[...]).astype(o_ref.dtype)


def rmsnorm(x, gain, tm):
    m, d = x.shape
    return pl.pallas_call(
        _rmsnorm_kernel,
        grid=(pl.cdiv(m, tm),),
        in_specs=[pl.BlockSpec((tm, d), lambda i: (i, 0)), pl.BlockSpec((1, d), lambda i: (0, 0))],
        out_specs=pl.BlockSpec((tm, d), lambda i: (i, 0)),
        out_shape=jax.ShapeDtypeStruct((m, d), BF16),
        compiler_params=_params("parallel"),
        name="rmsnorm",
    )(x, gain.reshape(1, d))


def _row_tile_spec(x, tm, row0, tile_of, **kwargs):
    if x.ndim == 2:
        return pl.BlockSpec((tm, x.shape[1]), lambda *g: (tile_of(*g), 0), **kwargs), x.shape[0]
    n_seq, t_full, d = x.shape
    t = t_full - row0
    tiles_per_seq = t // tm
    assert t % tm == 0 and row0 % 16 == 0

    def index(*g):
        i = tile_of(*g)
        return (i // tiles_per_seq, pl.multiple_of(row0 + (i % tiles_per_seq) * tm, 16), 0)

    return pl.BlockSpec((pl.Element(1), pl.Element(tm), pl.Element(d)), index, **kwargs), n_seq * t


def _ffn_in_kernel(x_ref, wg_ref, wu_ref, o_ref):
    x = x_ref[...].reshape(x_ref.shape[-2:])
    g = jnp.dot(x, wg_ref[...].astype(BF16), preferred_element_type=F32)
    u = jnp.dot(x, wu_ref[...].astype(BF16), preferred_element_type=F32)
    o_ref[...] = (_silu(g) * u).astype(o_ref.dtype)


def ffn_in(x, w_in, layer, which, tm, tn, row0=0):
    d = x.shape[-1]
    d_ff = w_in.shape[-1] // 2
    ncol = d_ff // tn
    assert d_ff % tn == 0
    xspec, m = _row_tile_spec(x, tm, row0, lambda j, i: i)
    wspec = lambda off: pl.BlockSpec((None, None, d, tn), lambda j, i: (layer, which, 0, j + off))
    return pl.pallas_call(
        _ffn_in_kernel,
        grid=(ncol, pl.cdiv(m, tm)),
        in_specs=[xspec, wspec(0), wspec(ncol)],
        out_specs=pl.BlockSpec((tm, tn), lambda j, i: (i, j)),
        out_shape=jax.ShapeDtypeStruct((m, d_ff), BF16),
        compiler_params=_params("parallel", "parallel"),
        name="ffn_in",
    )(x, w_in, w_in)


def _mm_res_norm_kernel(a_ref, w_ref, h_ref, g_ref, oh_ref, *on_refs, scale, nk):
    k = pl.program_id(1)

    @pl.when(k == 0)
    def _():
        oh_ref[...] = h_ref[...].reshape(oh_ref.shape)

    oh_ref[...] += scale * _bdot(a_ref[...], w_ref[...])

    @pl.when(k == nk - 1)
    def _():
        h = oh_ref[...]
        hs = h * _rms_scale(h)
        for n, on in enumerate(on_refs):
            on[...] = (hs * g_ref[n:n + 1, :]).astype(on.dtype)


def _res_norm_tiles(m, kdim, d, a_dtype, w_dtype, out_dtypes, tm_target):
    tm = _row_tile(m, min(tm_target, RES_TM_MAX))
    size = lambda dt: jnp.dtype(dt).itemsize
    fixed = 4 * tm * d * 4 + tm * d * 4
    fixed += sum(2 * tm * d * size(dt) for dt in out_dtypes)
    tk = LANES
    for cand in range(LANES, kdim + 1, LANES):
        per_k = 2 * cand * (d * size(w_dtype) + tm * size(a_dtype))
        if kdim % cand == 0 and fixed + per_k <= VMEM_BUDGET:
            tk = cand
    return tm, tk


def mm_res_norm(a, w, w_idx, h, gains, out_dtypes, scale, tm_target, row0=0):
    m, kdim = a.shape
    d = h.shape[-1]
    tm, tk = _res_norm_tiles(m, kdim, d, a.dtype, w.dtype, out_dtypes, tm_target)
    nk = kdim // tk
    n_norm = gains.shape[0]
    lead = (None,) * len(w_idx)
    row = lambda i, k: (i, 0)
    hspec, h_rows = _row_tile_spec(h, tm, row0, lambda i, k: i)
    assert h_rows == m
    outs = pl.pallas_call(
        functools.partial(_mm_res_norm_kernel, scale=scale, nk=nk),
        grid=(pl.cdiv(m, tm), nk),
        in_specs=[
            pl.BlockSpec((tm, tk), lambda i, k: (i, k)),
            pl.BlockSpec(lead + (tk, d), lambda i, k: tuple(w_idx) + (k, 0)),
            hspec,
            pl.BlockSpec((n_norm, d), lambda i, k: (0, 0)),
        ],
        out_specs=[pl.BlockSpec((tm, d), row)] * (1 + n_norm),
        out_shape=[jax.ShapeDtypeStruct((m, d), F32)]
        + [jax.ShapeDtypeStruct((m, d), dt) for dt in out_dtypes],
        compiler_params=_params("parallel", "arbitrary"),
        name="mm_res_norm",
    )(a, w, h, gains)
    return outs[0], outs[1:]


def _mm_kernel(x_ref, w_ref, o_ref):
    o_ref[...] = _bdot(x_ref[...], w_ref[...]).astype(o_ref.dtype)


def mm(x, w, w_idx, col0, ncols, tm, tn):
    m, kdim = x.shape
    assert col0 % tn == 0 and ncols % tn == 0
    lead = (None,) * len(w_idx)
    c0 = col0 // tn
    return pl.pallas_call(
        _mm_kernel,
        grid=(ncols // tn, pl.cdiv(m, tm)),
        in_specs=[pl.BlockSpec((tm, kdim), lambda j, i: (i, 0)),
                  pl.BlockSpec(lead + (kdim, tn), lambda j, i: tuple(w_idx) + (0, c0 + j))],
        out_specs=pl.BlockSpec((tm, tn), lambda j, i: (i, j)),
        out_shape=jax.ShapeDtypeStruct((m, ncols), F32),
        compiler_params=_params("parallel", "parallel"),
        name="mm",
    )(x, w)


def _mm_nt_kernel(x_ref, wt_ref, o_ref):
    o_ref[...] = _bdot_nt(x_ref[...], wt_ref[...]).astype(o_ref.dtype)


def mm_nt(x, wt, layer, row0, nrows, tm, tn):
    m, kdim = x.shape
    assert row0 % tn == 0 and nrows % tn == 0
    r0 = row0 // tn
    return pl.pallas_call(
        _mm_nt_kernel,
        grid=(nrows // tn, pl.cdiv(m, tm)),
        in_specs=[pl.BlockSpec((tm, kdim), lambda j, i: (i, 0)),
                  pl.BlockSpec((None, tn, kdim), lambda j, i: (layer, r0 + j, 0))],
        out_specs=pl.BlockSpec((tm, tn), lambda j, i: (i, j)),
        out_shape=jax.ShapeDtypeStruct((m, nrows), F32),
        compiler_params=_params("parallel", "parallel"),
        name="mm_nt",
    )(x, wt)


def _gdn_ba_kernel(x_ref, w_ref, alog_ref, dtb_ref, o_ref):
    r = _bdot_nt(x_ref[...], w_ref[...])[:, :o_ref.shape[1]]
    nh = r.shape[1] // 2
    lane = lax.broadcasted_iota(jnp.int32, r.shape, 1)
    xa = r + dtb_ref[...]
    softplus = jnp.maximum(xa, 0.0) + jnp.log1p(jnp.exp(-jnp.abs(xa)))
    g = -jnp.exp(alog_ref[...]) * softplus
    o_ref[...] = jnp.where(lane < nh, jax.nn.sigmoid(r), g)


def gdn_ba(x, wt, layer, col0, a_log, dt_bias, tm):
    m, d = x.shape
    nh = a_log.shape[0]
    assert col0 % LANES == 0 and wt.shape[1] - col0 == 2 * nh <= LANES
    pad = jnp.zeros((nh,), F32)
    alog = jnp.concatenate([pad, a_log.astype(F32)]).reshape(1, 2 * nh)
    dtb = jnp.concatenate([pad, dt_bias.astype(F32)]).reshape(1, 2 * nh)
    return pl.pallas_call(
        _gdn_ba_kernel,
        grid=(pl.cdiv(m, tm),),
        in_specs=[pl.BlockSpec((tm, d), lambda i: (i, 0)),
                  pl.BlockSpec((None, LANES, d), lambda i: (layer, col0 // LANES, 0)),
                  pl.BlockSpec((1, 2 * nh), lambda i: (0, 0)), pl.BlockSpec((1, 2 * nh), lambda i: (0, 0))],
        out_specs=pl.BlockSpec((tm, 2 * nh), lambda i: (i, 0)),
        out_shape=jax.ShapeDtypeStruct((m, 2 * nh), F32),
        compiler_params=_params("parallel"),
        name="gdn_ba",
    )(x, wt, alog, dtb)


def _l2norm(x):
    return x * lax.rsqrt(jnp.sum(x * x, axis=-1, keepdims=True) + L2_EPS)


def _col(x, idx):
    lane = lax.broadcasted_iota(jnp.int32, x.shape, 1)
    return jnp.sum(jnp.where(lane == idx, x, 0.0), axis=-1, keepdims=True)


def _row_of(col):
    r = col.shape[0]
    return jnp.broadcast_to(col, (r, max(r, 128))).T[:r, :]


def _unit_lower_inverse_minus_eye(a_list, n_sq):
    p = [-a for a in a_list]
    t = list(p)
    for _ in range(n_sq):
        p = [_bdot(x, x) for x in p]
        tp = [_bdot(x, y) for x, y in zip(t, p)]
        t = [x + y + z for x, y, z in zip(t, p, tp)]
    return t


def _delta_wy(kk, qk, k, q, v, beta, gc, tril, strict, n_sq):
    n = len(v)
    decay = [jnp.exp(jnp.where(tril, gc[i] - _row_of(gc[i]), -jnp.inf)) for i in range(n)]
    a = [jnp.where(strict, beta[i] * kk[i] * decay[i], 0.0) for i in range(n)]
    t = _unit_lower_inverse_minus_eye(a, n_sq)
    egc = [jnp.exp(g) for g in gc]
    rhs = [jnp.concatenate([v[i] * beta[i], k[i] * (beta[i] * egc[i])], axis=1) for i in range(n)]
    trhs = [_bdot(t[i], rhs[i]) for i in range(n)]
    return ([rhs[i] + trhs[i] for i in range(n)], [q[i] * egc[i] for i in range(n)],
            [qk[i] * decay[i] for i in range(n)])


def _gdn_prompt_kernel(q_ref, k_ref, v_ref, z_ref, bg_ref, wq_ref, wk_ref, wv_ref, ng_ref,
                       o_ref, s_out_ref, qwin, kwin, vwin, s_ref, *, seq_len, rows, chunk, g_heads, n_sq):
    h = pl.program_id(1)
    c = pl.program_id(2)
    nc = pl.num_programs(2)
    dk = GDN_HEAD
    nb = rows // chunk
    lead = 8

    @pl.when(c == 0)
    def _():
        s_ref[...] = jnp.zeros_like(s_ref)
        for win in (qwin, kwin, vwin):
            win[pl.ds(0, lead), :] = jnp.zeros((lead, win.shape[1]), F32)

    row = lax.broadcasted_iota(jnp.int32, (rows, 1), 0)
    valid = (c * rows + row) < seq_len

    def conv(win, x_ref, w_ref):
        win[pl.ds(lead, rows), :] = x_ref[0]
        acc = win[pl.ds(lead - 3, rows), :] * w_ref[0:1, :]
        for tap in range(1, GDN_CONV_TAPS):
            acc = acc + win[pl.ds(lead - 3 + tap, rows), :] * w_ref[tap:tap + 1, :]
        win[pl.ds(lead - 3, 3), :] = win[pl.ds(lead + rows - 3, 3), :]
        return jnp.where(valid, _silu(acc), 0.0)

    qc = conv(qwin, q_ref, wq_ref)
    kc = conv(kwin, k_ref, wk_ref)
    vc = conv(vwin, v_ref, wv_ref)
    bg = jnp.where(valid, bg_ref[0], 0.0)
    nh = bg.shape[1] // 2

    ri = lax.broadcasted_iota(jnp.int32, (rows, rows), 0)
    ci = lax.broadcasted_iota(jnp.int32, (rows, rows), 1)
    same = (ri // chunk) == (ci // chunk)
    tril = same & (ri >= ci)
    strict = same & (ri > ci)
    gc_all = _exact_left_dot(tril, bg)

    nv = 2 * g_heads
    qs = [_l2norm(qc[:, gi * dk:(gi + 1) * dk]) * (dk ** -0.5) for gi in range(g_heads)]
    ks = [_l2norm(kc[:, gi * dk:(gi + 1) * dk]) for gi in range(g_heads)]
    kks = [_bdot_nt(x, x) for x in ks]
    qks = [_bdot_nt(x, y) for x, y in zip(qs, ks)]
    per_v = lambda xs: [xs[sv // 2] for sv in range(nv)]
    q, k = per_v(qs), per_v(ks)
    v = [vc[:, sv * dk:(sv + 1) * dk] for sv in range(nv)]
    beta = [_col(bg, 2 * h * g_heads + sv) for sv in range(nv)]
    gc = [_col(gc_all, nh + 2 * h * g_heads + sv) for sv in range(nv)]
    uw, qg, qkd = _delta_wy(per_v(kks), per_v(qks), k, q, v, beta, gc, tril, strict, n_sq)
    ow = [_bdot(x, y) for x, y in zip(qkd, uw)]
    q_eff = [qg[i] - ow[i][:, dk:] for i in range(nv)]
    s = [s_ref[sv] for sv in range(nv)]
    o_parts = [[] for _ in range(nv)]
    for j in range(nb):
        sl = slice(j * chunk, (j + 1) * chunk)
        g_end = [g[(j + 1) * chunk - 1:(j + 1) * chunk, :] for g in gc]
        kd = [k[i][sl] * jnp.exp(g_end[i] - gc[i][sl]) for i in range(nv)]
        bn = [_bdot_tn(kd[i], uw[i][sl]) for i in range(nv)]
        qs_j = [_bdot(q_eff[i][sl], s[i]) for i in range(nv)]
        ns_j = [_bdot(bn[i][:, dk:], s[i]) for i in range(nv)]
        for i in range(nv):
            o_parts[i].append(qs_j[i] + ow[i][sl, :dk])
            s[i] = s[i] * jnp.exp(g_end[i]) - ns_j[i] + bn[i][:, :dk]
    outs = []
    for sv in range(nv):
        s_ref[sv] = s[sv]
        o = o_parts[sv][0] if nb == 1 else jnp.concatenate(o_parts[sv], axis=0)
        z = z_ref[0, :, sv * dk:(sv + 1) * dk]
        outs.append(o * _rms_scale(o) * ng_ref[...] * _silu(z))
    o_ref[0] = jnp.concatenate(outs, axis=1).astype(o_ref.dtype)

    @pl.when(c == nc - 1)
    def _():
        s_out_ref[0] = s_ref[...]


def gdn_prompt(raw, bg, conv_w, norm_g, rows, chunk, g_heads):
    b, t, _ = raw.shape
    nh = bg.shape[-1] // 2
    dk = GDN_HEAD
    n_qk = nh // 2
    qk_dim = n_qk * dk
    gq, gv = g_heads * dk, 2 * g_heads * dk
    n_hblk = n_qk // g_heads
    nc = pl.cdiv(t, rows)
    n_sq = int(math.log2(chunk)) - 1
    assert 2 ** (n_sq + 1) == chunk and n_qk % g_heads == 0 and rows % chunk == 0
    blk = lambda width, off: pl.BlockSpec((1, rows, width), lambda bi, h, c: (bi, c, off + h))
    wblk = lambda width, off: pl.BlockSpec((GDN_CONV_TAPS, width), lambda bi, h, c: (0, off + h))
    kernel = functools.partial(_gdn_prompt_kernel, seq_len=t, rows=rows, chunk=chunk, g_heads=g_heads,
                               n_sq=n_sq)
    return pl.pallas_call(
        kernel,
        grid=(b, n_hblk, nc),
        in_specs=[
            blk(gq, 0), blk(gq, n_hblk), blk(gv, n_hblk), blk(gv, 2 * n_hblk),
            pl.BlockSpec((1, rows, 2 * nh), lambda bi, h, c: (bi, c, 0)),
            wblk(gq, 0), wblk(gq, n_hblk), wblk(gv, n_hblk),
            pl.BlockSpec((1, dk), lambda bi, h, c: (0, 0)),
        ],
        out_specs=[
            pl.BlockSpec((1, rows, gv), lambda bi, h, c: (bi, c, h)),
            pl.BlockSpec((1, 2 * g_heads, dk, dk), lambda bi, h, c: (bi, h, 0, 0)),
        ],
        out_shape=[jax.ShapeDtypeStruct((b, t, 2 * qk_dim), BF16),
                   jax.ShapeDtypeStruct((b, nh, dk, dk), F32)],
        scratch_shapes=[pltpu.VMEM((8 + rows, gq), F32), pltpu.VMEM((8 + rows, gq), F32),
                        pltpu.VMEM((8 + rows, gv), F32), pltpu.VMEM((2 * g_heads, dk, dk), F32)],
        compiler_params=_params("parallel", "parallel", "arbitrary"),
        name="gdn_prompt",
    )(raw, raw, raw, raw, bg, conv_w, conv_w, conv_w, norm_g.reshape(1, dk))


def _gdn_sample_kernel(q_ref, k_ref, v_ref, z_ref, bg_ref, cq_ref, ck_ref, cv_ref, wq_ref, wk_ref, wv_ref,
                       ng_ref, s_in_ref, o_ref, s_out_ref, qwin, kwin, vwin, *, n_sq):
    h = pl.program_id(0)
    tb, t, dk = q_ref.shape
    r = tb * t
    lead = 8

    def conv(win, x_ref, c_ref, w_ref):
        win[:, pl.ds(lead - 3, 3), :] = c_ref[...]
        win[:, pl.ds(lead, t), :] = x_ref[...]
        acc = win[:, pl.ds(lead - 3, t), :] * w_ref[0:1, :]
        for i in range(1, 4):
            acc = acc + win[:, pl.ds(lead - 3 + i, t), :] * w_ref[i:i + 1, :]
        return _silu(acc).reshape(r, x_ref.shape[-1])

    qc = conv(qwin, q_ref, cq_ref, wq_ref)
    kc = conv(kwin, k_ref, ck_ref, wk_ref)
    vc = conv(vwin, v_ref, cv_ref, wv_ref)
    bg = bg_ref[...].reshape(r, bg_ref.shape[-1])
    nh = bg.shape[1] // 2

    ri = lax.broadcasted_iota(jnp.int32, (r, r), 0)
    ci = lax.broadcasted_iota(jnp.int32, (r, r), 1)
    same = (ri // t) == (ci // t)
    tril = same & (ri >= ci)
    strict = same & (ri > ci)
    gc_all = _exact_left_dot(tril, bg)
    g_sum = jnp.sum(bg_ref[...], axis=1, keepdims=True)

    q = _l2norm(qc) * (dk ** -0.5)
    k = _l2norm(kc)
    kk = _bdot_nt(k, k)
    qk = _bdot_nt(q, k)
    bdims = (((2,), (1,)), ((0,), (0,)))
    two = range(2)
    v = [vc[:, e * dk:(e + 1) * dk] for e in two]
    beta = [_col(bg, 2 * h + e) for e in two]
    gc = [_col(gc_all, nh + 2 * h + e) for e in two]
    uw, qg, qkd = _delta_wy([kk, kk], [qk, qk], [k, k], [q, q], v, beta, gc, tril, strict, n_sq)
    s = [s_in_ref[:, e] for e in two]
    wq3 = [jnp.concatenate([uw[e][:, dk:].reshape(tb, t, dk), qg[e].reshape(tb, t, dk)], axis=1).astype(BF16)
           for e in two]
    ws_qs = [lax.dot_general(wq3[e], s[e].astype(BF16), bdims, preferred_element_type=F32) for e in two]
    v_new = [uw[e][:, :dk] - ws_qs[e][:, :t].reshape(r, dk) for e in two]
    qkv = [_bdot(qkd[e], v_new[e]) for e in two]
    lane3 = lax.broadcasted_iota(jnp.int32, g_sum.shape, 2)
    g_last = [jnp.sum(jnp.where(lane3 == nh + 2 * h + e, g_sum, 0.0), axis=-1, keepdims=True) for e in two]
    kd = [(k.reshape(tb, t, dk) * jnp.exp(g_last[e] - gc[e].reshape(tb, t, 1))).astype(BF16) for e in two]
    upd = [lax.dot_general(kd[e], v_new[e].reshape(tb, t, dk).astype(BF16), (((1,), (1,)), ((0,), (0,))),
                           preferred_element_type=F32) for e in two]
    for e in two:
        s_out_ref[:, e] = s[e] * jnp.exp(g_last[e]) + upd[e]
        o = ws_qs[e][:, t:].reshape(r, dk) + qkv[e]
        z = z_ref[:, :, e * dk:(e + 1) * dk].reshape(r, dk)
        o_ref[:, :, e * dk:(e + 1) * dk] = (o * _rms_scale(o) * ng_ref[...] * _silu(z)).reshape(tb, t, dk)


def gdn_sample(raw, bg, conv_state, ssm_state, conv_w, norm_g, tb):
    b, t, _ = raw.shape
    nh = bg.shape[-1] // 2
    dk = GDN_HEAD
    n_qk = nh // 2
    n_sq = int(math.log2(t)) - 1
    assert 2 ** (n_sq + 1) == t and b % tb == 0 and t == 8
    blk = lambda width, off: pl.BlockSpec((tb, t, width), lambda h, i: (i, 0, off + h))
    cblk = lambda width, off: pl.BlockSpec((tb, 3, width), lambda h, i: (i, 0, off + h))
    wblk = lambda width, off: pl.BlockSpec((4, width), lambda h, i: (0, off + h))
    sblk = pl.BlockSpec((tb, 2, dk, dk), lambda h, i: (i, h, 0, 0))
    return pl.pallas_call(
        functools.partial(_gdn_sample_kernel, n_sq=n_sq),
        grid=(n_qk, b // tb),
        in_specs=[
            blk(dk, 0), blk(dk, n_qk), blk(2 * dk, n_qk), blk(2 * dk, 2 * n_qk),
            pl.BlockSpec((tb, t, 2 * nh), lambda h, i: (i, 0, 0)),
            cblk(dk, 0), cblk(dk, n_qk), cblk(2 * dk, n_qk),
            wblk(dk, 0), wblk(dk, n_qk), wblk(2 * dk, n_qk),
            pl.BlockSpec((1, dk), lambda h, i: (0, 0)),
            sblk,
        ],
        out_specs=[pl.BlockSpec((tb, t, 2 * dk), lambda h, i: (i, 0, h)), sblk],
        out_shape=[jax.ShapeDtypeStruct((b, t, nh * dk), F32),
                   jax.ShapeDtypeStruct(ssm_state.shape, F32)],
        scratch_shapes=[pltpu.VMEM((tb, 8 + t, dk), F32), pltpu.VMEM((tb, 8 + t, dk), F32),
                        pltpu.VMEM((tb, 8 + t, 2 * dk), F32)],
        compiler_params=_params("parallel", "parallel"),
        name="gdn_sample",
    )(raw, raw, raw, raw, bg, conv_state, conv_state, conv_state, conv_w, conv_w, conv_w,
      norm_g.reshape(1, dk), ssm_state)


def _t5_bucket(dist):
    max_exact = NUM_BUCKETS // 2
    n = jnp.maximum(dist, 0)
    log_ratio = jnp.log(jnp.maximum(n, 1).astype(F32) * (1.0 / max_exact)) * (1.0 / math.log(MAX_DISTANCE / max_exact))
    large = jnp.minimum(max_exact + (log_ratio * (NUM_BUCKETS - max_exact)).astype(jnp.int32), NUM_BUCKETS - 1)
    return jnp.where(n < max_exact, n, large)


def _prompt_bias_kernel(rb_ref, o_ref, *, tq, tk, n_meta):
    ty = pl.program_id(0)
    m = pl.program_id(1)
    ii = lax.broadcasted_iota(jnp.int32, (tq, tk), 0)
    jj = lax.broadcasted_iota(jnp.int32, (tq, tk), 1)
    dist = n_meta + (ty - 1) * tk + ii - jj
    bucket = _t5_bucket(dist)
    tile = jnp.full((tq, tk), NEG_BIG, F32)
    for bkt in range(NUM_BUCKETS):
        tile = jnp.where((bucket == bkt) & (dist >= 0), rb_ref[bkt, m], tile)
    o_ref[0, 0] = tile


def prompt_bias_tiles(rel_bias, tq, tk, n_meta):
    n_maps = rel_bias.shape[1]
    return pl.pallas_call(
        functools.partial(_prompt_bias_kernel, tq=tq, tk=tk, n_meta=n_meta),
        grid=(3, n_maps),
        in_specs=[pl.BlockSpec(memory_space=pltpu.SMEM)],
        out_specs=pl.BlockSpec((1, 1, tq, tk), lambda ty, m: (ty, m, 0, 0)),
        out_shape=jax.ShapeDtypeStruct((3, n_maps, tq, tk), F32),
        compiler_params=_params("parallel", "parallel"),
        name="prompt_bias",
    )(rel_bias.astype(F32))


def _lambda_of(lv_ref, lam_init):
    lv = lv_ref[...].astype(F32)
    a = jnp.sum(lv[0:1] * lv[1:2], axis=-1, keepdims=True)
    b = jnp.sum(lv[2:3] * lv[3:4], axis=-1, keepdims=True)
    return jnp.exp(a) - jnp.exp(b) + lam_init


def _attn_prompt_kernel(qi_ref, kj_ref, q_ref, k_ref, v_ref, bias_ref, rb_ref, lv_ref, sub_ref, o_ref,
                        m_ref, l_ref, acc_ref, *, n_keys, nk, tk, hp, lam_init, scale):
    h = pl.program_id(1)
    step = pl.program_id(2)
    qi = qi_ref[step]
    kj = kj_ref[step]
    dh = DIFF_HEAD
    hw = 2 * dh

    @pl.when(kj == 0)
    def _():
        m_ref[...] = jnp.full_like(m_ref, NEG_BIG)
        l_ref[...] = jnp.zeros_like(l_ref)
        acc_ref[...] = jnp.zeros_like(acc_ref)

    krow = lax.broadcasted_iota(jnp.int32, (tk, 1), 0)
    kvalid = (kj * tk + krow) < n_keys
    k = jnp.where(kvalid, k_ref[0], 0.0).astype(BF16)
    v = jnp.where(kvalid, v_ref[0], 0.0).astype(BF16)
    near = (qi - kj) <= 1
    lanes = m_ref.shape[-1]
    wide = lambda x, n: jnp.concatenate([x] * (n // lanes), axis=1)
    ones = jnp.ones((tk, lanes), BF16)
    maps = range(2 * hp)
    s = [_bdot_nt((q_ref[0, :, mi * dh:(mi + 1) * dh] * scale).astype(BF16), k[:, mi * dh:(mi + 1) * dh])
         for mi in maps]
    ps, corrs = [], []
    for mi in maps:
        far_bias = rb_ref[NUM_BUCKETS - 1, 2 * hp * h + mi]
        sb = s[mi] + jnp.where(near, bias_ref[0, mi], far_bias)
        m_old = m_ref[mi]
        m_new = jnp.maximum(m_old, jnp.max(sb, axis=-1, keepdims=True))
        m_ref[mi] = m_new
        corrs.append(jnp.exp(m_old - m_new))
        ps.append(jnp.exp(sb - wide(m_new, tk)).astype(BF16))
    sums = [jnp.dot(ps[mi], ones, preferred_element_type=F32) for mi in maps]
    pvs = [jnp.dot(ps[mi], v[:, (mi // 2) * hw:(mi // 2 + 1) * hw], preferred_element_type=F32) for mi in maps]
    for mi in maps:
        l_ref[mi] = l_ref[mi] * corrs[mi] + sums[mi]
        acc_ref[mi] = acc_ref[mi] * wide(corrs[mi], hw) + pvs[mi]

    @pl.when(kj == jnp.minimum(qi + 1, nk - 1))
    def _():
        lam = _lambda_of(lv_ref, lam_init)
        for hh in range(hp):
            o = (acc_ref[2 * hh] / wide(l_ref[2 * hh], hw)
                 - lam * (acc_ref[2 * hh + 1] / wide(l_ref[2 * hh + 1], hw)))
            o_ref[0, :, hh * hw:(hh + 1) * hw] = (
                o * _rms_scale(o) * sub_ref[...] * (1.0 - lam_init)).astype(o_ref.dtype)


def attn_prompt(q, k, v, bias_tiles, rel_bias, lam_vecs, subln, lam_init, tq, tk, hp):
    b, s, d = q.shape
    n_keys = k.shape[1]
    n_heads = d // (2 * DIFF_HEAD)
    hw = 2 * DIFF_HEAD
    bw = hp * hw
    nq, nk = s // tq, pl.cdiv(n_keys, tk)
    assert s % tq == 0 and tq == tk and N_META <= tk and n_heads % hp == 0
    assert N_META + 2 * tk - (tk - 1) >= MAX_DISTANCE
    pairs = [(qi, kj) for qi in range(nq) for kj in range(min(qi + 1, nk - 1) + 1)]
    qi_tab = jnp.asarray([p[0] for p in pairs], jnp.int32)
    kj_tab = jnp.asarray([p[1] for p in pairs], jnp.int32)

    qmap = lambda bi, h, st, qt, kt: (bi, qt[st], h)
    kvmap = lambda bi, h, st, qt, kt: (bi, kt[st], h)
    bias_map = lambda bi, h, st, qt, kt: (jnp.clip(qt[st] - kt[st] + 1, 0, 2), h, 0, 0)
    const2 = lambda bi, h, st, qt, kt: (0, 0)
    kernel = functools.partial(_attn_prompt_kernel, n_keys=n_keys, nk=nk, tk=tk, hp=hp, lam_init=lam_init,
                               scale=DIFF_HEAD ** -0.5)
    grid_spec = pltpu.PrefetchScalarGridSpec(
        num_scalar_prefetch=2,
        grid=(b, n_heads // hp, len(pairs)),
        in_specs=[
            pl.BlockSpec((1, tq, bw), qmap),
            pl.BlockSpec((1, tk, bw), kvmap),
            pl.BlockSpec((1, tk, bw), kvmap),
            pl.BlockSpec((1, 2 * hp, tq, tk), bias_map),
            pl.BlockSpec(memory_space=pltpu.SMEM),
            pl.BlockSpec(lam_vecs.shape, const2),
            pl.BlockSpec((1, hw), const2),
        ],
        out_specs=pl.BlockSpec((1, tq, bw), qmap),
        scratch_shapes=[pltpu.VMEM((2 * hp, tq, LANES), F32), pltpu.VMEM((2 * hp, tq, LANES), F32),
                        pltpu.VMEM((2 * hp, tq, hw), F32)],
    )
    return pl.pallas_call(
        kernel,
        grid_spec=grid_spec,
        out_shape=jax.ShapeDtypeStruct((b, s, d), BF16),
        compiler_params=_params("parallel", "parallel", "arbitrary"),
        name="attn_prompt",
    )(qi_tab, kj_tab, q, k, v, bias_tiles, rel_bias.astype(F32), lam_vecs, subln.reshape(1, hw))


def _sample_bias_kernel(rb_ref, o_ref, *, t, page, past_len):
    r = o_ref.shape[1]
    ri = lax.broadcasted_iota(jnp.int32, (r, page), 0)
    jj = lax.broadcasted_iota(jnp.int32, (r, page), 1)
    tq = ri % t
    far = rb_ref[:, NUM_BUCKETS - 1:NUM_BUCKETS]
    o_ref[0] = jnp.broadcast_to(far, (r, page))
    for ty, dist, ok in ((1, page + tq - jj, None), (2, tq - jj, (jj <= tq) & (jj < t))):
        bucket = _t5_bucket(dist)
        tile = jnp.full((r, page), NEG_BIG, F32)
        for bkt in range(NUM_BUCKETS):
            hit = (bucket == bkt) if ok is None else ((bucket == bkt) & ok)
            tile = jnp.where(hit, rb_ref[:, bkt:bkt + 1], tile)
        o_ref[ty] = tile
    del past_len


def sample_bias_tiles(rel_bias, t, page, past_len):
    n_maps = rel_bias.shape[1]
    assert past_len % page == 0 and past_len - page >= MAX_DISTANCE
    rb_rows = jnp.repeat(rel_bias.astype(F32).T, t, axis=0)
    return pl.pallas_call(
        functools.partial(_sample_bias_kernel, t=t, page=page, past_len=past_len),
        out_shape=jax.ShapeDtypeStruct((3, n_maps * t, page), F32),
        compiler_params=pltpu.CompilerParams(vmem_limit_bytes=VMEM_LIMIT),
        name="sample_bias",
    )(rb_rows)


def _attn_sample_kernel(*refs, lam_init, scale, page, pps):
    pt_ref, q_ref = refs[0], refs[1]
    kc_refs, vlo_refs, vhi_refs = refs[2:2 + pps], refs[2 + pps:2 + 2 * pps], refs[2 + 2 * pps:2 + 3 * pps]
    (kn_ref, vn_ref, far_ref, bias_ref, nbias_ref, lv_ref, sub_ref, o_ref, m_ref, l_ref, acc_ref) = refs[2 + 3 * pps:]
    del pt_ref
    p = pl.program_id(1)
    n_steps = pl.num_programs(1)
    t, d = q_ref.shape[1], q_ref.shape[2]
    dh = DIFF_HEAD
    hw = 2 * dh
    n_maps = d // dh
    n_heads = n_maps // 2

    @pl.when(p == 0)
    def _():
        m_ref[...] = jnp.full_like(m_ref, NEG_BIG)
        l_ref[...] = jnp.zeros_like(l_ref)
        acc_ref[...] = jnp.zeros_like(acc_ref)

    def update(k_ofs, v_ofs, biases):
        s = jnp.concatenate(
            [jnp.concatenate([_bdot_nt(q_ref[0, :, m * dh:(m + 1) * dh] * scale, k_of(m)) for m in range(n_maps)],
                             axis=0) + bias for k_of, bias in zip(k_ofs, biases)], axis=1)
        m_old = m_ref[...]
        m_new = jnp.maximum(m_old, jnp.max(s, axis=-1, keepdims=True))
        corr = jnp.exp(m_old - m_new)
        pr = jnp.exp(s - m_new)
        l_ref[...] = l_ref[...] * corr + jnp.sum(pr, axis=-1, keepdims=True)
        pv = jnp.concatenate(
            [_bdot(pr[2 * h * t:(2 * h + 2) * t], jnp.concatenate([v_of(h) for v_of in v_ofs], axis=0))
             for h in range(n_heads)], axis=0)
        acc_ref[...] = acc_ref[...] * corr + pv
        m_ref[...] = m_new

    strided_k = lambda ref: (lambda m: ref[pl.ds(m, page, stride=n_maps), :])
    strided_v = lambda lo, hi: (lambda h: jnp.concatenate([lo[pl.ds(h, page, stride=n_heads), :],
                                                           hi[pl.ds(h, page, stride=n_heads), :]], axis=1))
    update([strided_k(r) for r in kc_refs], [strided_v(lo, hi) for lo, hi in zip(vlo_refs, vhi_refs)],
           [far_ref[0]] * (pps - 1) + [bias_ref[0]])

    @pl.when(p == n_steps - 1)
    def _():
        kn, vn = kn_ref[0], vn_ref[0]
        zk = jnp.zeros((page - t, dh), F32)
        zv = jnp.zeros((page - t, hw), F32)
        update([lambda m: jnp.concatenate([kn[:, m * dh:(m + 1) * dh], zk], axis=0)],
               [lambda h: jnp.concatenate([vn[:, h * hw:(h + 1) * hw], zv], axis=0)], [nbias_ref[0]])
        lam = _lambda_of(lv_ref, lam_init)
        for h in range(n_heads):
            r0 = 2 * h * t
            o0 = acc_ref[pl.ds(r0, t), :] / l_ref[pl.ds(r0, t), :]
            o1 = acc_ref[pl.ds(r0 + t, t), :] / l_ref[pl.ds(r0 + t, t), :]
            o = o0 - lam * o1
            o_ref[0, :, h * hw:(h + 1) * hw] = o * _rms_scale(o) * sub_ref[...] * (1.0 - lam_init)


def attn_sample(q, k_new, v_new, cache_k, cache_v, page_table, bias_tiles, lam_vecs, subln, lam_init, pps):
    b, t, d = q.shape
    n_pages = page_table.shape[1]
    dh = DIFF_HEAD
    hw = 2 * dh
    n_maps = d // dh
    page = cache_k.shape[1] // n_maps
    r = n_maps * t
    n_steps = n_pages // pps
    assert r == bias_tiles.shape[1] and page == bias_tiles.shape[2] and cache_v.shape[1] == page * n_maps // 2
    assert n_pages % pps == 0

    def cache_map(i, half):
        return lambda bi, p, pt: (pt[bi, p * pps + i], 0, half)

    row = lambda bi, p, pt: (bi, 0, 0)
    tile = lambda index: pl.BlockSpec((1, r, page), index)
    grid_spec = pltpu.PrefetchScalarGridSpec(
        num_scalar_prefetch=1,
        grid=(b, n_steps),
        in_specs=[pl.BlockSpec((1, t, d), row)]
        + [pl.BlockSpec((None, page * n_maps, dh), cache_map(i, 0)) for i in range(pps)]
        + [pl.BlockSpec((None, page * n_maps // 2, dh), cache_map(i, 0)) for i in range(pps)]
        + [pl.BlockSpec((None, page * n_maps // 2, dh), cache_map(i, 1)) for i in range(pps)]
        + [
            pl.BlockSpec((1, t, d), row),
            pl.BlockSpec((1, t, d), row),
            tile(lambda bi, p, pt: (0, 0, 0)),
            tile(lambda bi, p, pt: (jnp.where(p == n_steps - 1, 1, 0), 0, 0)),
            tile(lambda bi, p, pt: (2, 0, 0)),
            pl.BlockSpec(lam_vecs.shape, lambda bi, p, pt: (0, 0)),
            pl.BlockSpec((1, hw), lambda bi, p, pt: (0, 0)),
        ],
        out_specs=pl.BlockSpec((1, t, d), row),
        scratch_shapes=[pltpu.VMEM((r, 1), F32), pltpu.VMEM((r, 1), F32), pltpu.VMEM((r, hw), F32)],
    )
    return pl.pallas_call(
        functools.partial(_attn_sample_kernel, lam_init=lam_init, scale=dh ** -0.5, page=page, pps=pps),
        grid_spec=grid_spec,
        out_shape=jax.ShapeDtypeStruct((b, t, d), F32),
        compiler_params=_params("parallel", "arbitrary"),
        name="attn_sample",
    )(page_table, q, *([cache_k] * pps), *([cache_v] * (2 * pps)), k_new, v_new, bias_tiles, bias_tiles,
      bias_tiles, lam_vecs, subln.reshape(1, hw))


TN = 512
TN_MM = 1024
ATTN_TILE = 256
ATTN_HEADS_PER_STEP = 8
GDN_CHUNK = 64
GDN_ROWS = 128
GDN_HEADS_PER_STEP = 8
GDN_SEQS_PER_STEP = 16
PAGES_PER_STEP = 8


def _row_tile(m, target):
    best = 16
    for t in range(16, target + 1, 16):
        if m % t == 0:
            best = t
    assert m % best == 0
    return best


def _trunk(h, n_seq, n_drop, gdn_fn, attn_fn, kv_fn, ln_gain, ffn_w_in, ffn_w_out, gdn_w_out, kv_norm,
           diff_w_q, diff_w_out, final_norm, n_a, tm_target):
    depth = ln_gain.shape[0]
    d = h.shape[-1]
    tm = _row_tile(h.shape[0], tm_target)
    hn = rmsnorm(h, ln_gain[0, 0], tm)
    conv_states, ssm_states, kv = [], [], None
    y = None
    row0 = 0
    for i in range(depth):
        rows = h.shape[0] if h.ndim == 2 else h.shape[0] * (h.shape[1] - row0)
        tm = _row_tile(rows, tm_target)
        act = ffn_in(hn, ffn_w_in, i, 0, tm, TN, row0)
        h, (hn,) = mm_res_norm(act, ffn_w_out, (i, 0), h, ln_gain[i, 1][None], [BF16], 0.5, tm_target, row0)
        row0 = 0
        if i < n_a:
            mix, conv_new, ssm_new = gdn_fn(i, hn, tm)
            conv_states.append(conv_new)
            ssm_states.append(ssm_new)
            h, (hn,) = mm_res_norm(mix, gdn_w_out, (i,), h, ln_gain[i, 2][None], [BF16], 1.0, tm_target)
        else:
            j = i - n_a
            q = mm(hn, diff_w_q, (j,), 0, d, tm, TN_MM)
            att = attn_fn(j, i, q, kv)
            h, (hn,) = mm_res_norm(att, diff_w_out, (j,), h, ln_gain[i, 2][None], [BF16], 1.0, tm_target)
        act = ffn_in(hn, ffn_w_in, i, 1, tm, TN)
        if i == depth - 1:
            _, (y,) = mm_res_norm(act, ffn_w_out, (i, 1), h, final_norm[None], [F32], 0.5, tm_target)
        elif i == n_a - 1:
            gains = jnp.stack([kv_norm, ln_gain[i + 1, 0]])
            h, (hkv, hn) = mm_res_norm(act, ffn_w_out, (i, 1), h, gains, [BF16, BF16], 0.5, tm_target)
            kv = kv_fn(hkv, tm)
            if n_drop:
                t = h.shape[0] // n_seq
                assert (t - n_drop) % 16 == 0
                h, hn, row0 = h.reshape(n_seq, t, d), hn.reshape(n_seq, t, d), n_drop
        else:
            h, (hn,) = mm_res_norm(act, ffn_w_out, (i, 1), h, ln_gain[i + 1, 0][None], [BF16], 0.5, tm_target)
    return y, kv, jnp.stack(conv_states), jnp.stack(ssm_states)


def kernel(x_prompt, x_sample, cache_k, cache_v, state_conv, state_ssm, page_table, meta_tokens, ln_gain,
           ffn_w_in, ffn_w_out, gdn_w_in, gdn_conv, gdn_a_log, gdn_dt_bias, gdn_norm, gdn_w_out, kv_norm,
           w_kv, diff_w_q, diff_lambda, diff_subln, diff_w_out, rel_bias, final_norm):
    b, seq, d = x_prompt.shape
    db, dseq, _ = x_sample.shape
    n_a = gdn_w_in.shape[0]
    n_heads_v = gdn_a_log.shape[1]
    qk_dim = (n_heads_v // 2) * GDN_HEAD
    conv_dim = gdn_conv.shape[-1]
    main_cols = conv_dim + n_heads_v * GDN_HEAD
    n_meta = meta_tokens.shape[0]
    t_prompt = n_meta + seq
    kv_cols = w_kv.shape[1] // 2
    n_maps = rel_bias.shape[1]
    page = cache_k.shape[1]
    past_len = page_table.shape[1] * page
    assert n_meta == N_META and conv_dim == 2 * qk_dim + n_heads_v * GDN_HEAD
    lam_init = lambda layer: 0.8 - 0.6 * math.exp(-0.3 * layer)

    gdn_w_in_t = jnp.swapaxes(gdn_w_in, 1, 2)

    def gdn_proj(i, hn, tm):
        raw = mm_nt(hn, gdn_w_in_t, i, 0, main_cols, tm, TN_MM)
        bg = gdn_ba(hn, gdn_w_in_t, i, main_cols, gdn_a_log[i], gdn_dt_bias[i], tm)
        return raw, bg

    def kv_proj(hkv, tm):
        return (mm(hkv, w_kv, (), 0, kv_cols, tm, TN_MM), mm(hkv, w_kv, (), kv_cols, kv_cols, tm, TN_MM))

    def gdn_prompt_fn(i, hn, tm):
        raw, bg = gdn_proj(i, hn, tm)
        raw3 = raw.reshape(b, t_prompt, main_cols)
        o, ssm = gdn_prompt(raw3, bg.reshape(b, t_prompt, -1), gdn_conv[i], gdn_norm[i], GDN_ROWS, GDN_CHUNK,
                            GDN_HEADS_PER_STEP)
        return o.reshape(b * t_prompt, -1), raw3[:, t_prompt - GDN_CONV_TAPS + 1:, :conv_dim], ssm

    prompt_bias = prompt_bias_tiles(rel_bias, ATTN_TILE, ATTN_TILE, n_meta)

    def attn_prompt_fn(j, layer, q, kv):
        k, v = kv
        o = attn_prompt(q.reshape(b, seq, d), k.reshape(b, t_prompt, kv_cols), v.reshape(b, t_prompt, kv_cols),
                        prompt_bias, rel_bias, diff_lambda[j], diff_subln[j], lam_init(layer), ATTN_TILE, ATTN_TILE,
                        ATTN_HEADS_PER_STEP)
        return o.reshape(b * seq, d)

    meta = jnp.broadcast_to(meta_tokens.astype(x_prompt.dtype)[None], (b, n_meta, d))
    h_prompt = jnp.concatenate([meta, x_prompt], axis=1).reshape(b * t_prompt, d)
    shared = (ln_gain, ffn_w_in, ffn_w_out.astype(BF16), gdn_w_out.astype(BF16), kv_norm, diff_w_q,
              diff_w_out.astype(BF16), final_norm, n_a)
    y_p, (k_p, v_p), conv_p, ssm_p = _trunk(h_prompt, b, n_meta, gdn_prompt_fn, attn_prompt_fn, kv_proj,
                                            *shared, tm_target=1408)

    def gdn_sample_fn(i, hn, tm):
        raw, bg = gdn_proj(i, hn, tm)
        raw3 = raw.reshape(db, dseq, main_cols)
        o, ssm = gdn_sample(raw3, bg.reshape(db, dseq, -1), state_conv[i], state_ssm[i], gdn_conv[i],
                            gdn_norm[i], GDN_SEQS_PER_STEP)
        return o.reshape(db * dseq, -1), raw3[:, dseq - 3:, :conv_dim], ssm

    sample_bias = sample_bias_tiles(rel_bias, dseq, page, past_len)
    cache_k2 = cache_k.reshape(cache_k.shape[0], page * cache_k.shape[2], cache_k.shape[3])
    cache_v2 = cache_v.reshape(cache_v.shape[0], page * cache_v.shape[2], cache_v.shape[3])

    def attn_sample_fn(j, layer, q, kv):
        k, v = kv
        r3 = lambda x: x.reshape(db, dseq, kv_cols)
        o = attn_sample(r3(q), r3(k), r3(v), cache_k2, cache_v2, page_table, sample_bias, diff_lambda[j],
                        diff_subln[j], lam_init(layer), PAGES_PER_STEP)
        return o.reshape(db * dseq, d)

    y_s, (k_s, v_s), conv_s, ssm_s = _trunk(x_sample.reshape(db * dseq, d), db, 0, gdn_sample_fn,
                                            attn_sample_fn, kv_proj, *shared, tm_target=1024)

    hd = DIFF_HEAD
    return (y_p.reshape(b, seq, d), y_s.reshape(db, dseq, d),
            k_p.reshape(b, t_prompt, n_maps, hd), v_p.reshape(b, t_prompt, n_maps // 2, 2 * hd),
            conv_p, ssm_p,
            k_s.reshape(db, dseq, n_maps, hd), v_s.reshape(db, dseq, n_maps // 2, 2 * hd),
            conv_s, ssm_s)
```

```python
import functools
import math

import jax
import jax.numpy as jnp
from jax import lax
from jax.experimental import pallas as pl
from jax.experimental.pallas import tpu as pltpu

F32 = jnp.float32
BF16 = jnp.bfloat16

EPS = 1e-6
L2_EPS = 1e-6
N_META = 16
GDN_HEAD = 128
GDN_CONV_TAPS = 4
DIFF_HEAD = 128
NUM_BUCKETS = 32
MAX_DISTANCE = 128
NEG_BIG = -1e30
LANES = 128
VMEM_LIMIT = 56 * 1024 * 1024
VMEM_BUDGET = 50 * 1024 * 1024
RES_TM_MAX = 768


def _params(*sem):
    return pltpu.CompilerParams(dimension_semantics=sem, vmem_limit_bytes=VMEM_LIMIT)


def _bdot(a, b):
    return jnp.dot(a.astype(BF16), b.astype(BF16), preferred_element_type=F32)


def _bdot_nt(a, b):
    return lax.dot_general(a.astype(BF16), b.astype(BF16), (((1,), (1,)), ((), ())),
                           preferred_element_type=F32)


def _bdot_tn(a, b):
    return lax.dot_general(a.astype(BF16), b.astype(BF16), (((0,), (0,)), ((), ())),
                           preferred_element_type=F32)


def _hdot(a, b):
    return jnp.dot(a, b, precision=lax.Precision.HIGHEST, preferred_element_type=F32)


def _split3(x):
    hi = x.astype(BF16)
    r1 = x - hi.astype(F32)
    mid = r1.astype(BF16)
    lo = (r1 - mid.astype(F32)).astype(BF16)
    return hi, mid, lo


def _exact_left_dot(mask01, x):
    m = mask01.astype(BF16)
    hi, mid, lo = _split3(x)
    d = functools.partial(jnp.dot, preferred_element_type=F32)
    return d(m, hi) + d(m, mid) + d(m, lo)


def _silu(x):
    return x * jax.nn.sigmoid(x)


def _rms_scale(x):
    return lax.rsqrt(jnp.mean(x * x, axis=-1, keepdims=True) + EPS)


def _rmsnorm_kernel(x_ref, g_ref, o_ref):
    x = x_ref[...].astype(F32)
    o_ref[...] = (x * _rms_scale(x) * g_ref[...]).astype(o_ref.dtype)


def rmsnorm(x, gain, tm):
    m, d = x.shape
    return pl.pallas_call(
        _rmsnorm_kernel,
        grid=(pl.cdiv(m, tm),),
        in_specs=[pl.BlockSpec((tm, d), lambda i: (i, 0)), pl.BlockSpec((1, d), lambda i: (0, 0))],
        out_specs=pl.BlockSpec((tm, d), lambda i: (i, 0)),
        out_shape=jax.ShapeDtypeStruct((m, d), BF16),
        compiler_params=_params("parallel"),
        name="rmsnorm",
    )(x, gain.reshape(1, d))


def _row_tile_spec(x, tm, row0, tile_of, **kwargs):
    if x.ndim == 2:
        return pl.BlockSpec((tm, x.shape[1]), lambda *g: (tile_of(*g), 0), **kwargs), x.shape[0]
    n_seq, t_full, d = x.shape
    t = t_full - row0
    tiles_per_seq = t // tm
    assert t % tm == 0 and row0 % 16 == 0

    def index(*g):
        i = tile_of(*g)
        return (i // tiles_per_seq, pl.multiple_of(row0 + (i % tiles_per_seq) * tm, 16), 0)

    return pl.BlockSpec((pl.Element(1), pl.Element(tm), pl.Element(d)), index, **kwargs), n_seq * t


def _ffn_in_kernel(x_ref, wg_ref, wu_ref, o_ref):
    x = x_ref[...].reshape(x_ref.shape[-2:])
    g = jnp.dot(x, wg_ref[...].astype(BF16), preferred_element_type=F32)
    u = jnp.dot(x, wu_ref[...].astype(BF16), preferred_element_type=F32)
    o_ref[...] = (_silu(g) * u).astype(o_ref.dtype)


def ffn_in(x, w_in, layer, which, tm, tn, row0=0):
    d = x.shape[-1]
    d_ff = w_in.shape[-1] // 2
    ncol = d_ff // tn
    assert d_ff % tn == 0
    xspec, m = _row_tile_spec(x, tm, row0, lambda j, i: i)
    wspec = lambda off: pl.BlockSpec((None, None, d, tn), lambda j, i: (layer, which, 0, j + off))
    return pl.pallas_call(
        _ffn_in_kernel,
        grid=(ncol, pl.cdiv(m, tm)),
        in_specs=[xspec, wspec(0), wspec(ncol)],
        out_specs=pl.BlockSpec((tm, tn), lambda j, i: (i, j)),
        out_shape=jax.ShapeDtypeStruct((m, d_ff), BF16),
        compiler_params=_params("parallel", "parallel"),
        name="ffn_in",
    )(x, w_in, w_in)


def _mm_res_norm_kernel(a_ref, w_ref, h_ref, g_ref, oh_ref, *on_refs, scale, nk):
    k = pl.program_id(1)

    @pl.when(k == 0)
    def _():
        oh_ref[...] = h_ref[...].reshape(oh_ref.shape)

    oh_ref[...] += scale * _bdot(a_ref[...], w_ref[...])

    @pl.when(k == nk - 1)
    def _():
        h = oh_ref[...]
        hs = h * _rms_scale(h)
        for n, on in enumerate(on_refs):
            on[...] = (hs * g_ref[n:n + 1, :]).astype(on.dtype)


def _res_norm_tiles(m, kdim, d, a_dtype, w_dtype, out_dtypes, tm_target):
    tm = _row_tile(m, min(tm_target, RES_TM_MAX))
    size = lambda dt: jnp.dtype(dt).itemsize
    fixed = 4 * tm * d * 4 + tm * d * 4
    fixed += sum(2 * tm * d * size(dt) for dt in out_dtypes)
    tk = LANES
    for cand in range(LANES, kdim + 1, LANES):
        per_k = 2 * cand * (d * size(w_dtype) + tm * size(a_dtype))
        if kdim % cand == 0 and fixed + per_k <= VMEM_BUDGET:
            tk = cand
    return tm, tk


def mm_res_norm(a, w, w_idx, h, gains, out_dtypes, scale, tm_target, row0=0):
    m, kdim = a.shape
    d = h.shape[-1]
    tm, tk = _res_norm_tiles(m, kdim, d, a.dtype, w.dtype, out_dtypes, tm_target)
    nk = kdim // tk
    n_norm = gains.shape[0]
    lead = (None,) * len(w_idx)
    row = lambda i, k: (i, 0)
    hspec, h_rows = _row_tile_spec(h, tm, row0, lambda i, k: i)
    assert h_rows == m
    outs = pl.pallas_call(
        functools.partial(_mm_res_norm_kernel, scale=scale, nk=nk),
        grid=(pl.cdiv(m, tm), nk),
        in_specs=[
            pl.BlockSpec((tm, tk), lambda i, k: (i, k)),
            pl.BlockSpec(lead + (tk, d), lambda i, k: tuple(w_idx) + (k, 0)),
            hspec,
            pl.BlockSpec((n_norm, d), lambda i, k: (0, 0)),
        ],
        out_specs=[pl.BlockSpec((tm, d), row)] * (1 + n_norm),
        out_shape=[jax.ShapeDtypeStruct((m, d), F32)]
        + [jax.ShapeDtypeStruct((m, d), dt) for dt in out_dtypes],
        compiler_params=_params("parallel", "arbitrary"),
        name="mm_res_norm",
    )(a, w, h, gains)
    return outs[0], outs[1:]


def _mm_kernel(x_ref, w_ref, o_ref):
    o_ref[...] = _bdot(x_ref[...], w_ref[...]).astype(o_ref.dtype)


def mm(x, w, w_idx, col0, ncols, tm, tn):
    m, kdim = x.shape
    assert col0 % tn == 0 and ncols % tn == 0
    lead = (None,) * len(w_idx)
    c0 = col0 // tn
    return pl.pallas_call(
        _mm_kernel,
        grid=(ncols // tn, pl.cdiv(m, tm)),
        in_specs=[pl.BlockSpec((tm, kdim), lambda j, i: (i, 0)),
                  pl.BlockSpec(lead + (kdim, tn), lambda j, i: tuple(w_idx) + (0, c0 + j))],
        out_specs=pl.BlockSpec((tm, tn), lambda j, i: (i, j)),
        out_shape=jax.ShapeDtypeStruct((m, ncols), F32),
        compiler_params=_params("parallel", "parallel"),
        name="mm",
    )(x, w)


def _mm_nt_kernel(x_ref, wt_ref, o_ref):
    o_ref[...] = _bdot_nt(x_ref[...], wt_ref[...]).astype(o_ref.dtype)


def mm_nt(x, wt, layer, row0, nrows, tm, tn):
    m, kdim = x.shape
    assert row0 % tn == 0 and nrows % tn == 0
    r0 = row0 // tn
    return pl.pallas_call(
        _mm_nt_kernel,
        grid=(nrows // tn, pl.cdiv(m, tm)),
        in_specs=[pl.BlockSpec((tm, kdim), lambda j, i: (i, 0)),
                  pl.BlockSpec((None, tn, kdim), lambda j, i: (layer, r0 + j, 0))],
        out_specs=pl.BlockSpec((tm, tn), lambda j, i: (i, j)),
        out_shape=jax.ShapeDtypeStruct((m, nrows), F32),
        compiler_params=_params("parallel", "parallel"),
        name="mm_nt",
    )(x, wt)


def _gdn_ba_kernel(x_ref, w_ref, alog_ref, dtb_ref, o_ref):
    r = _bdot_nt(x_ref[...], w_ref[...])[:, :o_ref.shape[1]]
    nh = r.shape[1] // 2
    lane = lax.broadcasted_iota(jnp.int32, r.shape, 1)
    xa = r + dtb_ref[...]
    softplus = jnp.maximum(xa, 0.0) + jnp.log1p(jnp.exp(-jnp.abs(xa)))
    g = -jnp.exp(alog_ref[...]) * softplus
    o_ref[...] = jnp.where(lane < nh, jax.nn.sigmoid(r), g)


def gdn_ba(x, wt, layer, col0, a_log, dt_bias, tm):
    m, d = x.shape
    nh = a_log.shape[0]
    assert col0 % LANES == 0 and wt.shape[1] - col0 == 2 * nh <= LANES
    pad = jnp.zeros((nh,), F32)
    alog = jnp.concatenate([pad, a_log.astype(F32)]).reshape(1, 2 * nh)
    dtb = jnp.concatenate([pad, dt_bias.astype(F32)]).reshape(1, 2 * nh)
    return pl.pallas_call(
        _gdn_ba_kernel,
        grid=(pl.cdiv(m, tm),),
        in_specs=[pl.BlockSpec((tm, d), lambda i: (i, 0)),
                  pl.BlockSpec((None, LANES, d), lambda i: (layer, col0 // LANES, 0)),
                  pl.BlockSpec((1, 2 * nh), lambda i: (0, 0)), pl.BlockSpec((1, 2 * nh), lambda i: (0, 0))],
        out_specs=pl.BlockSpec((tm, 2 * nh), lambda i: (i, 0)),
        out_shape=jax.ShapeDtypeStruct((m, 2 * nh), F32),
        compiler_params=_params("parallel"),
        name="gdn_ba",
    )(x, wt, alog, dtb)


def _l2norm(x):
    return x * lax.rsqrt(jnp.sum(x * x, axis=-1, keepdims=True) + L2_EPS)


def _col(x, idx):
    lane = lax.broadcasted_iota(jnp.int32, x.shape, 1)
    return jnp.sum(jnp.where(lane == idx, x, 0.0), axis=-1, keepdims=True)


def _row_of(col):
    r = col.shape[0]
    return jnp.broadcast_to(col, (r, max(r, 128))).T[:r, :]


def _unit_lower_inverse_minus_eye(a_list, n_sq):
    p = [-a for a in a_list]
    t = list(p)
    for _ in range(n_sq):
        p = [_bdot(x, x) for x in p]
        tp = [_bdot(x, y) for x, y in zip(t, p)]
        t = [x + y + z for x, y, z in zip(t, p, tp)]
    return t


def _delta_wy(kk, qk, k, q, v, beta, gc, tril, strict, n_sq):
    n = len(v)
    decay = [jnp.exp(jnp.where(tril, gc[i] - _row_of(gc[i]), -jnp.inf)) for i in range(n)]
    a = [jnp.where(strict, beta[i] * kk[i] * decay[i], 0.0) for i in range(n)]
    t = _unit_lower_inverse_minus_eye(a, n_sq)
    egc = [jnp.exp(g) for g in gc]
    rhs = [jnp.concatenate([v[i] * beta[i], k[i] * (beta[i] * egc[i])], axis=1) for i in range(n)]
    trhs = [_bdot(t[i], rhs[i]) for i in range(n)]
    return ([rhs[i] + trhs[i] for i in range(n)], [q[i] * egc[i] for i in range(n)],
            [qk[i] * decay[i] for i in range(n)])


def _gdn_prompt_kernel(q_ref, k_ref, v_ref, z_ref, bg_ref, wq_ref, wk_ref, wv_ref, ng_ref,
                       o_ref, s_out_ref, qwin, kwin, vwin, s_ref, *, seq_len, rows, chunk, g_heads, n_sq):
    h = pl.program_id(1)
    c = pl.program_id(2)
    nc = pl.num_programs(2)
    dk = GDN_HEAD
    nb = rows // chunk
    lead = 8

    @pl.when(c == 0)
    def _():
        s_ref[...] = jnp.zeros_like(s_ref)
        for win in (qwin, kwin, vwin):
            win[pl.ds(0, lead), :] = jnp.zeros((lead, win.shape[1]), F32)

    row = lax.broadcasted_iota(jnp.int32, (rows, 1), 0)
    valid = (c * rows + row) < seq_len

    def conv(win, x_ref, w_ref):
        win[pl.ds(lead, rows), :] = x_ref[0]
        acc = win[pl.ds(lead - 3, rows), :] * w_ref[0:1, :]
        for tap in range(1, GDN_CONV_TAPS):
            acc = acc + win[pl.ds(lead - 3 + tap, rows), :] * w_ref[tap:tap + 1, :]
        win[pl.ds(lead - 3, 3), :] = win[pl.ds(lead + rows - 3, 3), :]
        return jnp.where(valid, _silu(acc), 0.0)

    qc = conv(qwin, q_ref, wq_ref)
    kc = conv(kwin, k_ref, wk_ref)
    vc = conv(vwin, v_ref, wv_ref)
    bg = jnp.where(valid, bg_ref[0], 0.0)
    nh = bg.shape[1] // 2

    ri = lax.broadcasted_iota(jnp.int32, (rows, rows), 0)
    ci = lax.broadcasted_iota(jnp.int32, (rows, rows), 1)
    same = (ri // chunk) == (ci // chunk)
    tril = same & (ri >= ci)
    strict = same & (ri > ci)
    gc_all = _exact_left_dot(tril, bg)

    nv = 2 * g_heads
    qs = [_l2norm(qc[:, gi * dk:(gi + 1) * dk]) * (dk ** -0.5) for gi in range(g_heads)]
    ks = [_l2norm(kc[:, gi * dk:(gi + 1) * dk]) for gi in range(g_heads)]
    kks = [_bdot_nt(x, x) for x in ks]
    qks = [_bdot_nt(x, y) for x, y in zip(qs, ks)]
    per_v = lambda xs: [xs[sv // 2] for sv in range(nv)]
    q, k = per_v(qs), per_v(ks)
    v = [vc[:, sv * dk:(sv + 1) * dk] for sv in range(nv)]
    beta = [_col(bg, 2 * h * g_heads + sv) for sv in range(nv)]
    gc = [_col(gc_all, nh + 2 * h * g_heads + sv) for sv in range(nv)]
    uw, qg, qkd = _delta_wy(per_v(kks), per_v(qks), k, q, v, beta, gc, tril, strict, n_sq)
    ow = [_bdot(x, y) for x, y in zip(qkd, uw)]
    q_eff = [qg[i] - ow[i][:, dk:] for i in range(nv)]
    s = [s_ref[sv] for sv in range(nv)]
    o_parts = [[] for _ in range(nv)]
    for j in range(nb):
        sl = slice(j * chunk, (j + 1) * chunk)
        g_end = [g[(j + 1) * chunk - 1:(j + 1) * chunk, :] for g in gc]
        kd = [k[i][sl] * jnp.exp(g_end[i] - gc[i][sl]) for i in range(nv)]
        bn = [_bdot_tn(kd[i], uw[i][sl]) for i in range(nv)]
        qs_j = [_bdot(q_eff[i][sl], s[i]) for i in range(nv)]
        ns_j = [_bdot(bn[i][:, dk:], s[i]) for i in range(nv)]
        for i in range(nv):
            o_parts[i].append(qs_j[i] + ow[i][sl, :dk])
            s[i] = s[i] * jnp.exp(g_end[i]) - ns_j[i] + bn[i][:, :dk]
    outs = []
    for sv in range(nv):
        s_ref[sv] = s[sv]
        o = o_parts[sv][0] if nb == 1 else jnp.concatenate(o_parts[sv], axis=0)
        z = z_ref[0, :, sv * dk:(sv + 1) * dk]
        outs.append(o * _rms_scale(o) * ng_ref[...] * _silu(z))
    o_ref[0] = jnp.concatenate(outs, axis=1).astype(o_ref.dtype)

    @pl.when(c == nc - 1)
    def _():
        s_out_ref[0] = s_ref[...]


def gdn_prompt(raw, bg, conv_w, norm_g, rows, chunk, g_heads):
    b, t, _ = raw.shape
    nh = bg.shape[-1] // 2
    dk = GDN_HEAD
    n_qk = nh // 2
    qk_dim = n_qk * dk
    gq, gv = g_heads * dk, 2 * g_heads * dk
    n_hblk = n_qk // g_heads
    nc = pl.cdiv(t, rows)
    n_sq = int(math.log2(chunk)) - 1
    assert 2 ** (n_sq + 1) == chunk and n_qk % g_heads == 0 and rows % chunk == 0
    blk = lambda width, off: pl.BlockSpec((1, rows, width), lambda bi, h, c: (bi, c, off + h))
    wblk = lambda width, off: pl.BlockSpec((GDN_CONV_TAPS, width), lambda bi, h, c: (0, off + h))
    kernel = functools.partial(_gdn_prompt_kernel, seq_len=t, rows=rows, chunk=chunk, g_heads=g_heads,
                               n_sq=n_sq)
    return pl.pallas_call(
        kernel,
        grid=(b, n_hblk, nc),
        in_specs=[
            blk(gq, 0), blk(gq, n_hblk), blk(gv, n_hblk), blk(gv, 2 * n_hblk),
            pl.BlockSpec((1, rows, 2 * nh), lambda bi, h, c: (bi, c, 0)),
            wblk(gq, 0), wblk(gq, n_hblk), wblk(gv, n_hblk),
            pl.BlockSpec((1, dk), lambda bi, h, c: (0, 0)),
        ],
        out_specs=[
            pl.BlockSpec((1, rows, gv), lambda bi, h, c: (bi, c, h)),
            pl.BlockSpec((1, 2 * g_heads, dk, dk), lambda bi, h, c: (bi, h, 0, 0)),
        ],
        out_shape=[jax.ShapeDtypeStruct((b, t, 2 * qk_dim), BF16),
                   jax.ShapeDtypeStruct((b, nh, dk, dk), F32)],
        scratch_shapes=[pltpu.VMEM((8 + rows, gq), F32), pltpu.VMEM((8 + rows, gq), F32),
                        pltpu.VMEM((8 + rows, gv), F32), pltpu.VMEM((2 * g_heads, dk, dk), F32)],
        compiler_params=_params("parallel", "parallel", "arbitrary"),
        name="gdn_prompt",
    )(raw, raw, raw, raw, bg, conv_w, conv_w, conv_w, norm_g.reshape(1, dk))


def _gdn_sample_kernel(q_ref, k_ref, v_ref, z_ref, bg_ref, cq_ref, ck_ref, cv_ref, wq_ref, wk_ref, wv_ref,
                       ng_ref, s_in_ref, o_ref, s_out_ref, qwin, kwin, vwin, *, n_sq):
    h = pl.program_id(0)
    tb, t, dk = q_ref.shape
    r = tb * t
    lead = 8

    def conv(win, x_ref, c_ref, w_ref):
        win[:, pl.ds(lead - 3, 3), :] = c_ref[...]
        win[:, pl.ds(lead, t), :] = x_ref[...]
        acc = win[:, pl.ds(lead - 3, t), :] * w_ref[0:1, :]
        for i in range(1, 4):
            acc = acc + win[:, pl.ds(lead - 3 + i, t), :] * w_ref[i:i + 1, :]
        return _silu(acc).reshape(r, x_ref.shape[-1])

    qc = conv(qwin, q_ref, cq_ref, wq_ref)
    kc = conv(kwin, k_ref, ck_ref, wk_ref)
    vc = conv(vwin, v_ref, cv_ref, wv_ref)
    bg = bg_ref[...].reshape(r, bg_ref.shape[-1])
    nh = bg.shape[1] // 2

    ri = lax.broadcasted_iota(jnp.int32, (r, r), 0)
    ci = lax.broadcasted_iota(jnp.int32, (r, r), 1)
    same = (ri // t) == (ci // t)
    tril = same & (ri >= ci)
    strict = same & (ri > ci)
    gc_all = _exact_left_dot(tril, bg)
    g_sum = jnp.sum(bg_ref[...], axis=1, keepdims=True)

    q = _l2norm(qc) * (dk ** -0.5)
    k = _l2norm(kc)
    kk = _bdot_nt(k, k)
    qk = _bdot_nt(q, k)
    bdims = (((2,), (1,)), ((0,), (0,)))
    two = range(2)
    v = [vc[:, e * dk:(e + 1) * dk] for e in two]
    beta = [_col(bg, 2 * h + e) for e in two]
    gc = [_col(gc_all, nh + 2 * h + e) for e in two]
    uw, qg, qkd = _delta_wy([kk, kk], [qk, qk], [k, k], [q, q], v, beta, gc, tril, strict, n_sq)
    s = [s_in_ref[:, e] for e in two]
    wq3 = [jnp.concatenate([uw[e][:, dk:].reshape(tb, t, dk), qg[e].reshape(tb, t, dk)], axis=1).astype(BF16)
           for e in two]
    ws_qs = [lax.dot_general(wq3[e], s[e].astype(BF16), bdims, preferred_element_type=F32) for e in two]
    v_new = [uw[e][:, :dk] - ws_qs[e][:, :t].reshape(r, dk) for e in two]
    qkv = [_bdot(qkd[e], v_new[e]) for e in two]
    lane3 = lax.broadcasted_iota(jnp.int32, g_sum.shape, 2)
    g_last = [jnp.sum(jnp.where(lane3 == nh + 2 * h + e, g_sum, 0.0), axis=-1, keepdims=True) for e in two]
    kd = [(k.reshape(tb, t, dk) * jnp.exp(g_last[e] - gc[e].reshape(tb, t, 1))).astype(BF16) for e in two]
    upd = [lax.dot_general(kd[e], v_new[e].reshape(tb, t, dk).astype(BF16), (((1,), (1,)), ((0,), (0,))),
                           preferred_element_type=F32) for e in two]
    for e in two:
        s_out_ref[:, e] = s[e] * jnp.exp(g_last[e]) + upd[e]
        o = ws_qs[e][:, t:].reshape(r, dk) + qkv[e]
        z = z_ref[:, :, e * dk:(e + 1) * dk].reshape(r, dk)
        o_ref[:, :, e * dk:(e + 1) * dk] = (o * _rms_scale(o) * ng_ref[...] * _silu(z)).reshape(tb, t, dk)


def gdn_sample(raw, bg, conv_state, ssm_state, conv_w, norm_g, tb):
    b, t, _ = raw.shape
    nh = bg.shape[-1] // 2
    dk = GDN_HEAD
    n_qk = nh // 2
    n_sq = int(math.log2(t)) - 1
    assert 2 ** (n_sq + 1) == t and b % tb == 0 and t == 8
    blk = lambda width, off: pl.BlockSpec((tb, t, width), lambda h, i: (i, 0, off + h))
    cblk = lambda width, off: pl.BlockSpec((tb, 3, width), lambda h, i: (i, 0, off + h))
    wblk = lambda width, off: pl.BlockSpec((4, width), lambda h, i: (0, off + h))
    sblk = pl.BlockSpec((tb, 2, dk, dk), lambda h, i: (i, h, 0, 0))
    return pl.pallas_call(
        functools.partial(_gdn_sample_kernel, n_sq=n_sq),
        grid=(n_qk, b // tb),
        in_specs=[
            blk(dk, 0), blk(dk, n_qk), blk(2 * dk, n_qk), blk(2 * dk, 2 * n_qk),
            pl.BlockSpec((tb, t, 2 * nh), lambda h, i: (i, 0, 0)),
            cblk(dk, 0), cblk(dk, n_qk), cblk(2 * dk, n_qk),
            wblk(dk, 0), wblk(dk, n_qk), wblk(2 * dk, n_qk),
            pl.BlockSpec((1, dk), lambda h, i: (0, 0)),
            sblk,
        ],
        out_specs=[pl.BlockSpec((tb, t, 2 * dk), lambda h, i: (i, 0, h)), sblk],
        out_shape=[jax.ShapeDtypeStruct((b, t, nh * dk), F32),
                   jax.ShapeDtypeStruct(ssm_state.shape, F32)],
        scratch_shapes=[pltpu.VMEM((tb, 8 + t, dk), F32), pltpu.VMEM((tb, 8 + t, dk), F32),
                        pltpu.VMEM((tb, 8 + t, 2 * dk), F32)],
        compiler_params=_params("parallel", "parallel"),
        name="gdn_sample",
    )(raw, raw, raw, raw, bg, conv_state, conv_state, conv_state, conv_w, conv_w, conv_w,
      norm_g.reshape(1, dk), ssm_state)


def _t5_bucket(dist):
    max_exact = NUM_BUCKETS // 2
    n = jnp.maximum(dist, 0)
    log_ratio = jnp.log(jnp.maximum(n, 1).astype(F32) * (1.0 / max_exact)) * (1.0 / math.log(MAX_DISTANCE / max_exact))
    large = jnp.minimum(max_exact + (log_ratio * (NUM_BUCKETS - max_exact)).astype(jnp.int32), NUM_BUCKETS - 1)
    return jnp.where(n < max_exact, n, large)


def _prompt_bias_kernel(rb_ref, o_ref, *, tq, tk, n_meta):
    ty = pl.program_id(0)
    m = pl.program_id(1)
    ii = lax.broadcasted_iota(jnp.int32, (tq, tk), 0)
    jj = lax.broadcasted_iota(jnp.int32, (tq, tk), 1)
    dist = n_meta + (ty - 1) * tk + ii - jj
    bucket = _t5_bucket(dist)
    tile = jnp.full((tq, tk), NEG_BIG, F32)
    for bkt in range(NUM_BUCKETS):
        tile = jnp.where((bucket == bkt) & (dist >= 0), rb_ref[bkt, m], tile)
    o_ref[0, 0] = tile


def prompt_bias_tiles(rel_bias, tq, tk, n_meta):
    n_maps = rel_bias.shape[1]
    return pl.pallas_call(
        functools.partial(_prompt_bias_kernel, tq=tq, tk=tk, n_meta=n_meta),
        grid=(3, n_maps),
        in_specs=[pl.BlockSpec(memory_space=pltpu.SMEM)],
        out_specs=pl.BlockSpec((1, 1, tq, tk), lambda ty, m: (ty, m, 0, 0)),
        out_shape=jax.ShapeDtypeStruct((3, n_maps, tq, tk), F32),
        compiler_params=_params("parallel", "parallel"),
        name="prompt_bias",
    )(rel_bias.astype(F32))


def _lambda_of(lv_ref, lam_init):
    lv = lv_ref[...].astype(F32)
    a = jnp.sum(lv[0:1] * lv[1:2], axis=-1, keepdims=True)
    b = jnp.sum(lv[2:3] * lv[3:4], axis=-1, keepdims=True)
    return jnp.exp(a) - jnp.exp(b) + lam_init


def _attn_prompt_kernel(qi_ref, kj_ref, q_ref, k_ref, v_ref, bias_ref, rb_ref, lv_ref, sub_ref, o_ref,
                        m_ref, l_ref, acc_ref, *, n_keys, nk, tk, hp, lam_init, scale):
    h = pl.program_id(1)
    step = pl.program_id(2)
    qi = qi_ref[step]
    kj = kj_ref[step]
    dh = DIFF_HEAD
    hw = 2 * dh

    @pl.when(kj == 0)
    def _():
        m_ref[...] = jnp.full_like(m_ref, NEG_BIG)
        l_ref[...] = jnp.zeros_like(l_ref)
        acc_ref[...] = jnp.zeros_like(acc_ref)

    krow = lax.broadcasted_iota(jnp.int32, (tk, 1), 0)
    kvalid = (kj * tk + krow) < n_keys
    k = jnp.where(kvalid, k_ref[0], 0.0).astype(BF16)
    v = jnp.where(kvalid, v_ref[0], 0.0).astype(BF16)
    near = (qi - kj) <= 1
    lanes = m_ref.shape[-1]
    wide = lambda x, n: jnp.concatenate([x] * (n // lanes), axis=1)
    ones = jnp.ones((tk, lanes), BF16)
    maps = range(2 * hp)
    s = [_bdot_nt((q_ref[0, :, mi * dh:(mi + 1) * dh] * scale).astype(BF16), k[:, mi * dh:(mi + 1) * dh])
         for mi in maps]
    ps, corrs = [], []
    for mi in maps:
        far_bias = rb_ref[NUM_BUCKETS - 1, 2 * hp * h + mi]
        sb = s[mi] + jnp.where(near, bias_ref[0, mi], far_bias)
        m_old = m_ref[mi]
        m_new = jnp.maximum(m_old, jnp.max(sb, axis=-1, keepdims=True))
        m_ref[mi] = m_new
        corrs.append(jnp.exp(m_old - m_new))
        ps.append(jnp.exp(sb - wide(m_new, tk)).astype(BF16))
    sums = [jnp.dot(ps[mi], ones, preferred_element_type=F32) for mi in maps]
    pvs = [jnp.dot(ps[mi], v[:, (mi // 2) * hw:(mi // 2 + 1) * hw], preferred_element_type=F32) for mi in maps]
    for mi in maps:
        l_ref[mi] = l_ref[mi] * corrs[mi] + sums[mi]
        acc_ref[mi] = acc_ref[mi] * wide(corrs[mi], hw) + pvs[mi]

    @pl.when(kj == jnp.minimum(qi + 1, nk - 1))
    def _():
        lam = _lambda_of(lv_ref, lam_init)
        for hh in range(hp):
            o = (acc_ref[2 * hh] / wide(l_ref[2 * hh], hw)
                 - lam * (acc_ref[2 * hh + 1] / wide(l_ref[2 * hh + 1], hw)))
            o_ref[0, :, hh * hw:(hh + 1) * hw] = (
                o * _rms_scale(o) * sub_ref[...] * (1.0 - lam_init)).astype(o_ref.dtype)


def attn_prompt(q, k, v, bias_tiles, rel_bias, lam_vecs, subln, lam_init, tq, tk, hp):
    b, s, d = q.shape
    n_keys = k.shape[1]
    n_heads = d // (2 * DIFF_HEAD)
    hw = 2 * DIFF_HEAD
    bw = hp * hw
    nq, nk = s // tq, pl.cdiv(n_keys, tk)
    assert s % tq == 0 and tq == tk and N_META <= tk and n_heads % hp == 0
    assert N_META + 2 * tk - (tk - 1) >= MAX_DISTANCE
    pairs = [(qi, kj) for qi in range(nq) for kj in range(min(qi + 1, nk - 1) + 1)]
    qi_tab = jnp.asarray([p[0] for p in pairs], jnp.int32)
    kj_tab = jnp.asarray([p[1] for p in pairs], jnp.int32)

    qmap = lambda bi, h, st, qt, kt: (bi, qt[st], h)
    kvmap = lambda bi, h, st, qt, kt: (bi, kt[st], h)
    bias_map = lambda bi, h, st, qt, kt: (jnp.clip(qt[st] - kt[st] + 1, 0, 2), h, 0, 0)
    const2 = lambda bi, h, st, qt, kt: (0, 0)
    kernel = functools.partial(_attn_prompt_kernel, n_keys=n_keys, nk=nk, tk=tk, hp=hp, lam_init=lam_init,
                               scale=DIFF_HEAD ** -0.5)
    grid_spec = pltpu.PrefetchScalarGridSpec(
        num_scalar_prefetch=2,
        grid=(b, n_heads // hp, len(pairs)),
        in_specs=[
            pl.BlockSpec((1, tq, bw), qmap),
            pl.BlockSpec((1, tk, bw), kvmap),
            pl.BlockSpec((1, tk, bw), kvmap),
            pl.BlockSpec((1, 2 * hp, tq, tk), bias_map),
            pl.BlockSpec(memory_space=pltpu.SMEM),
            pl.BlockSpec(lam_vecs.shape, const2),
            pl.BlockSpec((1, hw), const2),
        ],
        out_specs=pl.BlockSpec((1, tq, bw), qmap),
        scratch_shapes=[pltpu.VMEM((2 * hp, tq, LANES), F32), pltpu.VMEM((2 * hp, tq, LANES), F32),
                        pltpu.VMEM((2 * hp, tq, hw), F32)],
    )
    return pl.pallas_call(
        kernel,
        grid_spec=grid_spec,
        out_shape=jax.ShapeDtypeStruct((b, s, d), BF16),
        compiler_params=_params("parallel", "parallel", "arbitrary"),
        name="attn_prompt",
    )(qi_tab, kj_tab, q, k, v, bias_tiles, rel_bias.astype(F32), lam_vecs, subln.reshape(1, hw))


def _sample_bias_kernel(rb_ref, o_ref, *, t, page, past_len):
    r = o_ref.shape[1]
    ri = lax.broadcasted_iota(jnp.int32, (r, page), 0)
    jj = lax.broadcasted_iota(jnp.int32, (r, page), 1)
    tq = ri % t
    far = rb_ref[:, NUM_BUCKETS - 1:NUM_BUCKETS]
    o_ref[0] = jnp.broadcast_to(far, (r, page))
    for ty, dist, ok in ((1, page + tq - jj, None), (2, tq - jj, (jj <= tq) & (jj < t))):
        bucket = _t5_bucket(dist)
        tile = jnp.full((r, page), NEG_BIG, F32)
        for bkt in range(NUM_BUCKETS):
            hit = (bucket == bkt) if ok is None else ((bucket == bkt) & ok)
            tile = jnp.where(hit, rb_ref[:, bkt:bkt + 1], tile)
        o_ref[ty] = tile
    del past_len


def sample_bias_tiles(rel_bias, t, page, past_len):
    n_maps = rel_bias.shape[1]
    assert past_len % page == 0 and past_len - page >= MAX_DISTANCE
    rb_rows = jnp.repeat(rel_bias.astype(F32).T, t, axis=0)
    return pl.pallas_call(
        functools.partial(_sample_bias_kernel, t=t, page=page, past_len=past_len),
        out_shape=jax.ShapeDtypeStruct((3, n_maps * t, page), F32),
        compiler_params=pltpu.CompilerParams(vmem_limit_bytes=VMEM_LIMIT),
        name="sample_bias",
    )(rb_rows)


def _attn_sample_kernel(*refs, lam_init, scale, page, pps):
    pt_ref, q_ref = refs[0], refs[1]
    kc_refs, vlo_refs, vhi_refs = refs[2:2 + pps], refs[2 + pps:2 + 2 * pps], refs[2 + 2 * pps:2 + 3 * pps]
    (kn_ref, vn_ref, far_ref, bias_ref, nbias_ref, lv_ref, sub_ref, o_ref, m_ref, l_ref, acc_ref) = refs[2 + 3 * pps:]
    del pt_ref
    p = pl.program_id(1)
    n_steps = pl.num_programs(1)
    t, d = q_ref.shape[1], q_ref.shape[2]
    dh = DIFF_HEAD
    hw = 2 * dh
    n_maps = d // dh
    n_heads = n_maps // 2

    @pl.when(p == 0)
    def _():
        m_ref[...] = jnp.full_like(m_ref, NEG_BIG)
        l_ref[...] = jnp.zeros_like(l_ref)
        acc_ref[...] = jnp.zeros_like(acc_ref)

    def update(k_ofs, v_ofs, biases):
        s = jnp.concatenate(
            [jnp.concatenate([_bdot_nt(q_ref[0, :, m * dh:(m + 1) * dh] * scale, k_of(m)) for m in range(n_maps)],
                             axis=0) + bias for k_of, bias in zip(k_ofs, biases)], axis=1)
        m_old = m_ref[...]
        m_new = jnp.maximum(m_old, jnp.max(s, axis=-1, keepdims=True))
        corr = jnp.exp(m_old - m_new)
        pr = jnp.exp(s - m_new)
        l_ref[...] = l_ref[...] * corr + jnp.sum(pr, axis=-1, keepdims=True)
        pv = jnp.concatenate(
            [_bdot(pr[2 * h * t:(2 * h + 2) * t], jnp.concatenate([v_of(h) for v_of in v_ofs], axis=0))
             for h in range(n_heads)], axis=0)
        acc_ref[...] = acc_ref[...] * corr + pv
        m_ref[...] = m_new

    strided_k = lambda ref: (lambda m: ref[pl.ds(m, page, stride=n_maps), :])
    strided_v = lambda lo, hi: (lambda h: jnp.concatenate([lo[pl.ds(h, page, stride=n_heads), :],
                                                           hi[pl.ds(h, page, stride=n_heads), :]], axis=1))
    update([strided_k(r) for r in kc_refs], [strided_v(lo, hi) for lo, hi in zip(vlo_refs, vhi_refs)],
           [far_ref[0]] * (pps - 1) + [bias_ref[0]])

    @pl.when(p == n_steps - 1)
    def _():
        kn, vn = kn_ref[0], vn_ref[0]
        zk = jnp.zeros((page - t, dh), F32)
        zv = jnp.zeros((page - t, hw), F32)
        update([lambda m: jnp.concatenate([kn[:, m * dh:(m + 1) * dh], zk], axis=0)],
               [lambda h: jnp.concatenate([vn[:, h * hw:(h + 1) * hw], zv], axis=0)], [nbias_ref[0]])
        lam = _lambda_of(lv_ref, lam_init)
        for h in range(n_heads):
            r0 = 2 * h * t
            o0 = acc_ref[pl.ds(r0, t), :] / l_ref[pl.ds(r0, t), :]
            o1 = acc_ref[pl.ds(r0 + t, t), :] / l_ref[pl.ds(r0 + t, t), :]
            o = o0 - lam * o1
            o_ref[0, :, h * hw:(h + 1) * hw] = o * _rms_scale(o) * sub_ref[...] * (1.0 - lam_init)


def attn_sample(q, k_new, v_new, cache_k, cache_v, page_table, bias_tiles, lam_vecs, subln, lam_init, pps):
    b, t, d = q.shape
    n_pages = page_table.shape[1]
    dh = DIFF_HEAD
    hw = 2 * dh
    n_maps = d // dh
    page = cache_k.shape[1] // n_maps
    r = n_maps * t
    n_steps = n_pages // pps
    assert r == bias_tiles.shape[1] and page == bias_tiles.shape[2] and cache_v.shape[1] == page * n_maps // 2
    assert n_pages % pps == 0

    def cache_map(i, half):
        return lambda bi, p, pt: (pt[bi, p * pps + i], 0, half)

    row = lambda bi, p, pt: (bi, 0, 0)
    tile = lambda index: pl.BlockSpec((1, r, page), index)
    grid_spec = pltpu.PrefetchScalarGridSpec(
        num_scalar_prefetch=1,
        grid=(b, n_steps),
        in_specs=[pl.BlockSpec((1, t, d), row)]
        + [pl.BlockSpec((None, page * n_maps, dh), cache_map(i, 0)) for i in range(pps)]
        + [pl.BlockSpec((None, page * n_maps // 2, dh), cache_map(i, 0)) for i in range(pps)]
        + [pl.BlockSpec((None, page * n_maps // 2, dh), cache_map(i, 1)) for i in range(pps)]
        + [
            pl.BlockSpec((1, t, d), row),
            pl.BlockSpec((1, t, d), row),
            tile(lambda bi, p, pt: (0, 0, 0)),
            tile(lambda bi, p, pt: (jnp.where(p == n_steps - 1, 1, 0), 0, 0)),
            tile(lambda bi, p, pt: (2, 0, 0)),
            pl.BlockSpec(lam_vecs.shape, lambda bi, p, pt: (0, 0)),
            pl.BlockSpec((1, hw), lambda bi, p, pt: (0, 0)),
        ],
        out_specs=pl.BlockSpec((1, t, d), row),
        scratch_shapes=[pltpu.VMEM((r, 1), F32), pltpu.VMEM((r, 1), F32), pltpu.VMEM((r, hw), F32)],
    )
    return pl.pallas_call(
        functools.partial(_attn_sample_kernel, lam_init=lam_init, scale=dh ** -0.5, page=page, pps=pps),
        grid_spec=grid_spec,
        out_shape=jax.ShapeDtypeStruct((b, t, d), F32),
        compiler_params=_params("parallel", "arbitrary"),
        name="attn_sample",
    )(page_table, q, *([cache_k] * pps), *([cache_v] * (2 * pps)), k_new, v_new, bias_tiles, bias_tiles,
      bias_tiles, lam_vecs, subln.reshape(1, hw))


TN = 512
TN_MM = 1024
ATTN_TILE = 256
ATTN_HEADS_PER_STEP = 8
GDN_CHUNK = 64
GDN_ROWS = 128
GDN_HEADS_PER_STEP = 8
GDN_SEQS_PER_STEP = 16
PAGES_PER_STEP = 8


def _row_tile(m, target):
    best = 16
    for t in range(16, target + 1, 16):
        if m % t == 0:
            best = t
    assert m % best == 0
    return best


def _trunk(h, n_seq, n_drop, gdn_fn, attn_fn, kv_fn, ln_gain, ffn_w_in, ffn_w_out, gdn_w_out, kv_norm,
           diff_w_q, diff_w_out, final_norm, n_a, tm_target):
    depth = ln_gain.shape[0]
    d = h.shape[-1]
    tm = _row_tile(h.shape[0], tm_target)
    hn = rmsnorm(h, ln_gain[0, 0], tm)
    conv_states, ssm_states, kv = [], [], None
    y = None
    row0 = 0
    for i in range(depth):
        rows = h.shape[0] if h.ndim == 2 else h.shape[0] * (h.shape[1] - row0)
        tm = _row_tile(rows, tm_target)
        act = ffn_in(hn, ffn_w_in, i, 0, tm, TN, row0)
        h, (hn,) = mm_res_norm(act, ffn_w_out, (i, 0), h, ln_gain[i, 1][None], [BF16], 0.5, tm_target, row0)
        row0 = 0
        if i < n_a:
            mix, conv_new, ssm_new = gdn_fn(i, hn, tm)
            conv_states.append(conv_new)
            ssm_states.append(ssm_new)
            h, (hn,) = mm_res_norm(mix, gdn_w_out, (i,), h, ln_gain[i, 2][None], [BF16], 1.0, tm_target)
        else:
            j = i - n_a
            q = mm(hn, diff_w_q, (j,), 0, d, tm, TN_MM)
            att = attn_fn(j, i, q, kv)
            h, (hn,) = mm_res_norm(att, diff_w_out, (j,), h, ln_gain[i, 2][None], [BF16], 1.0, tm_target)
        act = ffn_in(hn, ffn_w_in, i, 1, tm, TN)
        if i == depth - 1:
            _, (y,) = mm_res_norm(act, ffn_w_out, (i, 1), h, final_norm[None], [F32], 0.5, tm_target)
        elif i == n_a - 1:
            gains = jnp.stack([kv_norm, ln_gain[i + 1, 0]])
            h, (hkv, hn) = mm_res_norm(act, ffn_w_out, (i, 1), h, gains, [BF16, BF16], 0.5, tm_target)
            kv = kv_fn(hkv, tm)
            if n_drop:
                t = h.shape[0] // n_seq
                assert (t - n_drop) % 16 == 0
                h, hn, row0 = h.reshape(n_seq, t, d), hn.reshape(n_seq, t, d), n_drop
        else:
            h, (hn,) = mm_res_norm(act, ffn_w_out, (i, 1), h, ln_gain[i + 1, 0][None], [BF16], 0.5, tm_target)
    return y, kv, jnp.stack(conv_states), jnp.stack(ssm_states)


def kernel(x_prompt, x_sample, cache_k, cache_v, state_conv, state_ssm, page_table, meta_tokens, ln_gain,
           ffn_w_in, ffn_w_out, gdn_w_in, gdn_conv, gdn_a_log, gdn_dt_bias, gdn_norm, gdn_w_out, kv_norm,
           w_kv, diff_w_q, diff_lambda, diff_subln, diff_w_out, rel_bias, final_norm):
    b, seq, d = x_prompt.shape
    db, dseq, _ = x_sample.shape
    n_a = gdn_w_in.shape[0]
    n_heads_v = gdn_a_log.shape[1]
    qk_dim = (n_heads_v // 2) * GDN_HEAD
    conv_dim = gdn_conv.shape[-1]
    main_cols = conv_dim + n_heads_v * GDN_HEAD
    n_meta = meta_tokens.shape[0]
    t_prompt = n_meta + seq
    kv_cols = w_kv.shape[1] // 2
    n_maps = rel_bias.shape[1]
    page = cache_k.shape[1]
    past_len = page_table.shape[1] * page
    assert n_meta == N_META and conv_dim == 2 * qk_dim + n_heads_v * GDN_HEAD
    lam_init = lambda layer: 0.8 - 0.6 * math.exp(-0.3 * layer)

    gdn_w_in_t = jnp.swapaxes(gdn_w_in, 1, 2)

    def gdn_proj(i, hn, tm):
        raw = mm_nt(hn, gdn_w_in_t, i, 0, main_cols, tm, TN_MM)
        bg = gdn_ba(hn, gdn_w_in_t, i, main_cols, gdn_a_log[i], gdn_dt_bias[i], tm)
        return raw, bg

    def kv_proj(hkv, tm):
        return (mm(hkv, w_kv, (), 0, kv_cols, tm, TN_MM), mm(hkv, w_kv, (), kv_cols, kv_cols, tm, TN_MM))

    def gdn_prompt_fn(i, hn, tm):
        raw, bg = gdn_proj(i, hn, tm)
        raw3 = raw.reshape(b, t_prompt, main_cols)
        o, ssm = gdn_prompt(raw3, bg.reshape(b, t_prompt, -1), gdn_conv[i], gdn_norm[i], GDN_ROWS, GDN_CHUNK,
                            GDN_HEADS_PER_STEP)
        return o.reshape(b * t_prompt, -1), raw3[:, t_prompt - GDN_CONV_TAPS + 1:, :conv_dim], ssm

    prompt_bias = prompt_bias_tiles(rel_bias, ATTN_TILE, ATTN_TILE, n_meta)

    def attn_prompt_fn(j, layer, q, kv):
        k, v = kv
        o = attn_prompt(q.reshape(b, seq, d), k.reshape(b, t_prompt, kv_cols), v.reshape(b, t_prompt, kv_cols),
                        prompt_bias, rel_bias, diff_lambda[j], diff_subln[j], lam_init(layer), ATTN_TILE, ATTN_TILE,
                        ATTN_HEADS_PER_STEP)
        return o.reshape(b * seq, d)

    meta = jnp.broadcast_to(meta_tokens.astype(x_prompt.dtype)[None], (b, n_meta, d))
    h_prompt = jnp.concatenate([meta, x_prompt], axis=1).reshape(b * t_prompt, d)
    shared = (ln_gain, ffn_w_in, ffn_w_out.astype(BF16), gdn_w_out.astype(BF16), kv_norm, diff_w_q,
              diff_w_out.astype(BF16), final_norm, n_a)
    y_p, (k_p, v_p), conv_p, ssm_p = _trunk(h_prompt, b, n_meta, gdn_prompt_fn, attn_prompt_fn, kv_proj,
                                            *shared, tm_target=1408)

    def gdn_sample_fn(i, hn, tm):
        raw, bg = gdn_proj(i, hn, tm)
        raw3 = raw.reshape(db, dseq, main_cols)
        o, ssm = gdn_sample(raw3, bg.reshape(db, dseq, -1), state_conv[i], state_ssm[i], gdn_conv[i],
                            gdn_norm[i], GDN_SEQS_PER_STEP)
        return o.reshape(db * dseq, -1), raw3[:, dseq - 3:, :conv_dim], ssm

    sample_bias = sample_bias_tiles(rel_bias, dseq, page, past_len)
    cache_k2 = cache_k.reshape(cache_k.shape[0], page * cache_k.shape[2], cache_k.shape[3])
    cache_v2 = cache_v.reshape(cache_v.shape[0], page * cache_v.shape[2], cache_v.shape[3])

    def attn_sample_fn(j, layer, q, kv):
        k, v = kv
        r3 = lambda x: x.reshape(db, dseq, kv_cols)
        o = attn_sample(r3(q), r3(k), r3(v), cache_k2, cache_v2, page_table, sample_bias, diff_lambda[j],
                        diff_subln[j], lam_init(layer), PAGES_PER_STEP)
        return o.reshape(db * dseq, d)

    y_s, (k_s, v_s), conv_s, ssm_s = _trunk(x_sample.reshape(db * dseq, d), db, 0, gdn_sample_fn,
                                            attn_sample_fn, kv_proj, *shared, tm_target=1024)

    hd = DIFF_HEAD
    return (y_p.reshape(b, seq, d), y_s.reshape(db, dseq, d),
            k_p.reshape(b, t_prompt, n_maps, hd), v_p.reshape(b, t_prompt, n_maps // 2, 2 * hd),
            conv_p, ssm_p,
            k_s.reshape(db, dseq, n_maps, hd), v_s.reshape(db, dseq, n_maps // 2, 2 * hd),
            conv_s, ssm_s)
```
